```python
import jax, jax.numpy as jnp
from jax import lax
import numpy as np

D_MODEL = 1024
BATCH = 8
SEQ = 4096
DEPTH = 4

PLE_DIM = 256
ROPE_THETA = 10000.0
NORM_EPS = 1e-6
BLOCK = 128
WINDOW = 128

A_HEADS = 8
A_KV_HEADS = 2
A_HEAD_DIM = 64
A_WIDTH = A_HEADS * A_HEAD_DIM

B_HEADS = 8
B_Q_LORA = 384
B_KV_LORA = 256
B_NOPE = 64
B_ROPE = 32
B_QK_DIM = B_NOPE + B_ROPE
B_V_DIM = 64
B_WIDTH = B_HEADS * B_V_DIM

MIX_WIDTH = A_WIDTH + B_WIDTH
IN_SIZES = (A_WIDTH, A_KV_HEADS * A_HEAD_DIM, A_KV_HEADS * A_HEAD_DIM, A_WIDTH,
            B_Q_LORA, B_KV_LORA, B_ROPE, B_WIDTH)
N_IN = 512 + 128 + 128 + 512 + 384 + 256 + 32 + 512

kernel_name = "hybrid_swa_sink_mla_parallel_heads"


def rmsnorm(x, g):
    xf = x.astype(jnp.float32)
    r = xf * lax.rsqrt(jnp.mean(xf * xf, axis=-1, keepdims=True) + NORM_EPS)
    return (r * g.astype(jnp.float32)).astype(x.dtype)


def rope(x, pos):
    d = x.shape[-1]
    half = d // 2
    inv = ROPE_THETA ** (-jnp.arange(half, dtype=jnp.float32) * 2.0 / d)
    ang = pos.astype(jnp.float32)[..., None] * inv
    cos = jnp.cos(ang)[:, :, None, :]
    sin = jnp.sin(ang)[:, :, None, :]
    xf = x.astype(jnp.float32)
    x1, x2 = xf[..., :half], xf[..., half:]
    return jnp.concatenate([x1 * cos - x2 * sin, x2 * cos + x1 * sin], axis=-1).astype(x.dtype)


def window_sink_attention(q, k, v, sink):
    B, S, H, d = q.shape
    KV = k.shape[2]
    G = H // KV
    nb = S // BLOCK
    scale = d ** -0.5
    pad = ((0, 0), (BLOCK, BLOCK), (0, 0), (0, 0))
    kp = jnp.pad(k, pad).reshape(B, nb + 2, BLOCK, KV, d)
    vp = jnp.pad(v, pad).reshape(B, nb + 2, BLOCK, KV, d)
    kb = jnp.concatenate([kp[:, :-2], kp[:, 1:-1], kp[:, 2:]], axis=2)
    vb = jnp.concatenate([vp[:, :-2], vp[:, 1:-1], vp[:, 2:]], axis=2)
    qb = q.reshape(B, nb, BLOCK, KV, G, d)
    s = jnp.einsum('bnqkgd,bnckd->bnkgqc', qb, kb).astype(jnp.float32) * scale
    qi = jnp.arange(BLOCK)[:, None]
    ci = jnp.arange(3 * BLOCK)[None, :]
    band = jnp.abs(ci - BLOCK - qi) <= WINDOW
    kpos = jnp.arange(nb)[:, None] * BLOCK - BLOCK + ci
    valid = (kpos >= 0) & (kpos < S)
    mask = band[None, :, :] & valid[:, None, :]
    s = jnp.where(mask[None, :, None, None, :, :], s, jnp.float32(-1e30))
    sink_l = jnp.broadcast_to(sink.astype(jnp.float32).reshape(1, 1, KV, G, 1, 1),
                              s.shape[:-1] + (1,))
    pr = jax.nn.softmax(jnp.concatenate([s, sink_l], axis=-1), axis=-1)[..., :-1]
    o = jnp.einsum('bnkgqc,bnckd->bnqkgd', pr.astype(v.dtype), vb)
    return o.reshape(B, S, H, d)


def dense_block_attention(q, k, v):
    B, S, H, d = q.shape
    nb = S // BLOCK
    scale = d ** -0.5
    qb = q.reshape(B, nb, BLOCK, H, d).transpose(1, 0, 2, 3, 4)

    def one_block(qblk):
        s = jnp.einsum('bqhd,bkhd->bhqk', qblk, k).astype(jnp.float32) * scale
        pr = jax.nn.softmax(s, axis=-1).astype(v.dtype)
        return jnp.einsum('bhqk,bkhd->bqhd', pr, v)

    o = lax.map(one_block, qb)
    return o.transpose(1, 0, 2, 3, 4).reshape(B, S, H, v.shape[-1])


def hybrid_layer(x, p_i, positions, norm_g, w_in, a_q_norm, a_k_norm, a_sink,
                 b_cq_norm, b_ckv_norm, b_w_uq, b_w_ukv, b_q_norm, b_k_norm,
                 w_out, ple_g, ple_w_gate, ple_w_proj):
    B, S, _ = x.shape
    h = rmsnorm(x, norm_g)
    z = h @ w_in
    idx = np.cumsum(np.array(IN_SIZES))[:-1].tolist()
    aq, ak, av, ag, bcq, bckv, bkr, bg = jnp.split(z, idx, axis=-1)

    qa = rope(rmsnorm(aq.reshape(B, S, A_HEADS, A_HEAD_DIM), a_q_norm), positions)
    ka = rope(rmsnorm(ak.reshape(B, S, A_KV_HEADS, A_HEAD_DIM), a_k_norm), positions)
    va = av.reshape(B, S, A_KV_HEADS, A_HEAD_DIM)
    oa = window_sink_attention(qa, ka, va, a_sink).reshape(B, S, A_WIDTH)
    oa = oa * jax.nn.silu(ag)

    qf = (rmsnorm(bcq, b_cq_norm) @ b_w_uq).reshape(B, S, B_HEADS, B_QK_DIM)
    kv = (rmsnorm(bckv, b_ckv_norm) @ b_w_ukv).reshape(B, S, B_HEADS, B_NOPE + B_V_DIM)
    k_nope, vb = kv[..., :B_NOPE], kv[..., B_NOPE:]
    k_rope = jnp.broadcast_to(bkr[:, :, None, :], (B, S, B_HEADS, B_ROPE))
    qf = rmsnorm(qf, b_q_norm)
    kf = rmsnorm(jnp.concatenate([k_nope, k_rope], axis=-1), b_k_norm)
    qb = jnp.concatenate([qf[..., :B_NOPE], rope(qf[..., B_NOPE:], positions)], axis=-1)
    kb = jnp.concatenate([kf[..., :B_NOPE], rope(kf[..., B_NOPE:], positions)], axis=-1)
    ob = dense_block_attention(qb, kb, vb).reshape(B, S, B_WIDTH)
    ob = ob * jax.nn.silu(bg)

    x = x + jnp.concatenate([oa, ob], axis=-1) @ w_out

    gate = jax.nn.sigmoid(rmsnorm(x, ple_g) @ ple_w_gate)
    return x + gate * (p_i @ ple_w_proj)


def setup_inputs(seed: int = 0) -> dict:
    key = jax.random.key(seed)
    ks = jax.random.split(key, 20)
    f32 = jnp.float32

    def nrm(k, shape, scale):
        return jax.random.normal(k, shape, f32) * scale

    def gain(k, shape):
        return 1.0 + 0.02 * jax.random.normal(k, shape, f32)

    out_scale = MIX_WIDTH ** -0.5 / np.sqrt(2.0 * DEPTH)
    return {
        "x": nrm(ks[0], (BATCH, SEQ, D_MODEL), 1.0),
        "p": nrm(ks[1], (DEPTH, BATCH, SEQ, PLE_DIM), 1.0),
        "positions": jnp.broadcast_to(jnp.arange(SEQ, dtype=jnp.int32)[None, :], (BATCH, SEQ)),
        "norm_g": gain(ks[2], (DEPTH, D_MODEL)),
        "w_in": nrm(ks[3], (DEPTH, D_MODEL, N_IN), D_MODEL ** -0.5),
        "a_q_norm": gain(ks[4], (DEPTH, A_HEAD_DIM)),
        "a_k_norm": gain(ks[5], (DEPTH, A_HEAD_DIM)),
        "a_sink": nrm(ks[6], (DEPTH, A_HEADS), 0.5),
        "b_cq_norm": gain(ks[7], (DEPTH, B_Q_LORA)),
        "b_ckv_norm": gain(ks[8], (DEPTH, B_KV_LORA)),
        "b_w_uq": nrm(ks[9], (DEPTH, B_Q_LORA, B_HEADS * B_QK_DIM), B_Q_LORA ** -0.5),
        "b_w_ukv": nrm(ks[10], (DEPTH, B_KV_LORA, B_HEADS * (B_NOPE + B_V_DIM)), B_KV_LORA ** -0.5),
        "b_q_norm": gain(ks[11], (DEPTH, B_QK_DIM)),
        "b_k_norm": gain(ks[12], (DEPTH, B_QK_DIM)),
        "w_out": nrm(ks[13], (DEPTH, MIX_WIDTH, D_MODEL), out_scale),
        "ple_g": gain(ks[14], (DEPTH, D_MODEL)),
        "ple_w_gate": nrm(ks[15], (DEPTH, D_MODEL, D_MODEL), D_MODEL ** -0.5),
        "ple_w_proj": nrm(ks[16], (DEPTH, PLE_DIM, D_MODEL), 0.5 * PLE_DIM ** -0.5),
    }


def reference(x, p, positions, norm_g, w_in, a_q_norm, a_k_norm, a_sink,
              b_cq_norm, b_ckv_norm, b_w_uq, b_w_ukv, b_q_norm, b_k_norm,
              w_out, ple_g, ple_w_gate, ple_w_proj):
    for i in range(DEPTH):
        x = hybrid_layer(x, p[i], positions, norm_g[i], w_in[i], a_q_norm[i], a_k_norm[i],
                         a_sink[i], b_cq_norm[i], b_ckv_norm[i], b_w_uq[i], b_w_ukv[i],
                         b_q_norm[i], b_k_norm[i], w_out[i], ple_g[i], ple_w_gate[i],
                         ple_w_proj[i])
    return x
```

```python
import functools
import math

import jax
import jax.numpy as jnp
from jax import lax
from jax.experimental import pallas as pl
from jax.experimental.pallas import tpu as pltpu

D_MODEL = 1024
PLE_DIM = 256
ROPE_THETA = 10000.0
NORM_EPS = 1e-6
BLOCK = 128
A_HEADS = 8
A_HEAD_DIM = 64
A_WIDTH = 512
B_HEADS = 8
B_Q_LORA = 384
B_KV_LORA = 256
B_NOPE = 64
B_ROPE = 32
B_QK_DIM = 96
B_V_DIM = 64
B_WIDTH = 512
LANES = 128
LOG2E = math.log2(math.e)
VT_ROWS = 80

O_AQ, O_AK, O_AV, O_AG, O_BG, O_CQ, O_CKV, O_KR, N_IN_P = 0, 512, 640, 768, 1280, 1792, 2176, 2432, 2560

VMEM_LIMIT = 56 * 1024 * 1024


def _bf(x):
    return x.astype(jnp.bfloat16)


def _dot(a, b):
    return jnp.dot(a, b, preferred_element_type=jnp.float32)


def _dot_nt(a, b):
    return lax.dot_general(a, b, (((1,), (1,)), ((), ())), preferred_element_type=jnp.float32)


def _table_kernel(pos_ref, inva_ref, sga_ref, invb_ref, sgb_ref, cosa_ref, sina_ref, cosb_ref, sinb_ref):
    pos = pos_ref[...].astype(jnp.float32)
    anga = pos * inva_ref[...]
    cosa_ref[...] = jnp.cos(anga)
    sina_ref[...] = jnp.sin(anga) * sga_ref[...]
    angb = pos * invb_ref[...]
    cosb_ref[...] = jnp.cos(angb)
    sinb_ref[...] = jnp.sin(angb) * sgb_ref[...]


def _rope_tables(positions):
    T = positions.size
    pos = positions.reshape(T, 1)
    lane = jnp.arange(LANES)
    half_a = A_HEAD_DIM // 2
    inv_a = ROPE_THETA ** (-jnp.arange(half_a, dtype=jnp.float32) * 2.0 / A_HEAD_DIM)
    inva_row = inv_a[lane % half_a][None, :]
    sga_row = jnp.where((lane % A_HEAD_DIM) < half_a, -1.0, 1.0).astype(jnp.float32)[None, :]
    half_b = B_ROPE // 2
    inv_b = ROPE_THETA ** (-jnp.arange(half_b, dtype=jnp.float32) * 2.0 / B_ROPE)
    in_rope = (lane >= B_NOPE) & (lane < B_QK_DIM)
    invb_row = jnp.where(in_rope, inv_b[(lane - B_NOPE) % half_b], 0.0).astype(jnp.float32)[None, :]
    sgb_row = jnp.where(in_rope, jnp.where(lane < B_NOPE + half_b, -1.0, 1.0), 0.0).astype(jnp.float32)[None, :]
    tt = 1024
    row = pl.BlockSpec((1, LANES), lambda i: (0, 0))
    tab = pl.BlockSpec((tt, LANES), lambda i: (i, 0))
    shp = jax.ShapeDtypeStruct((T, LANES), jnp.float32)
    return pl.pallas_call(
        _table_kernel,
        out_shape=(shp, shp, shp, shp),
        grid=(T // tt,),
        in_specs=[pl.BlockSpec((tt, 1), lambda i: (i, 0)), row, row, row, row],
        out_specs=(tab, tab, tab, tab),
        name="rope_tables",
    )(pos, inva_row, sga_row, invb_row, sgb_row)


def _group_sumsq(x, g_ref):
    x2 = x * x
    hi = _bf(x2)
    lo = _bf(x2 - hi.astype(jnp.float32))
    g = g_ref[...]
    return _dot(hi, g) + _dot(lo, g)


def _rope_a(x, cos, sin):
    lane = lax.broadcasted_iota(jnp.int32, x.shape, 1)
    up = pltpu.roll(x, LANES - A_HEAD_DIM // 2, 1)
    dn = pltpu.roll(x, A_HEAD_DIM // 2, 1)
    rot = jnp.where((lane % A_HEAD_DIM) < A_HEAD_DIM // 2, up, dn)
    return x * cos + rot * sin


def _rope_b(x, cos, sin):
    lane = lax.broadcasted_iota(jnp.int32, x.shape, 1)
    up = pltpu.roll(x, LANES - B_ROPE // 2, 1)
    dn = pltpu.roll(x, B_ROPE // 2, 1)
    rot = jnp.where(lane < B_NOPE + B_ROPE // 2, up, dn)
    return x * cos + rot * sin


def _rms(x, n):
    return x * lax.rsqrt(jnp.sum(x * x, axis=-1, keepdims=True) * (1.0 / n) + NORM_EPS)


def _proj_kernel(x_ref, ng_ref, win_ref, cosa_ref, sina_ref, cosb_ref, sinb_ref,
                 aqn_ref, akn_ref, cqn_ref, ckvn_ref, wuq_ref, wukv_ref, bqn_ref, bkn_ref,
                 g512_ref, g128_ref,
                 qa_ref, ka4_ref, va4_ref, gates_ref, qb_ref, kb_ref, vt_ref):
    x = x_ref[...]
    h = _rms(x, D_MODEL) * ng_ref[...]
    z = _dot(_bf(h), win_ref[...])
    cosa, sina = cosa_ref[...], sina_ref[...]
    cosb, sinb = cosb_ref[...], sinb_ref[...]

    aq = z[:, O_AQ:O_AQ + A_WIDTH]
    ssq = _group_sumsq(aq, g512_ref)
    aqn = aq * lax.rsqrt(ssq * (1.0 / A_HEAD_DIM) + NORM_EPS) * aqn_ref[...]
    qscale = (A_HEAD_DIM ** -0.5) * LOG2E
    for c in range(A_WIDTH // LANES):
        sl = slice(c * LANES, (c + 1) * LANES)
        qa_ref[:, sl] = _bf(_rope_a(aqn[:, sl], cosa, sina) * qscale)

    ak = z[:, O_AK:O_AK + LANES]
    ssk = _group_sumsq(ak, g128_ref)
    akn = ak * lax.rsqrt(ssk * (1.0 / A_HEAD_DIM) + NORM_EPS) * akn_ref[...]
    akr = _rope_a(akn, cosa, sina)
    av = z[:, O_AV:O_AV + LANES]
    lane = lax.broadcasted_iota(jnp.int32, ak.shape, 1)
    low = lane < A_HEAD_DIM
    for src, dst in ((akr, ka4_ref), (av, va4_ref)):
        lo_part = jnp.where(low, src, 0.0)
        hi_part = jnp.where(low, 0.0, src)
        dst[:, 0 * LANES:1 * LANES] = _bf(lo_part)
        dst[:, 1 * LANES:2 * LANES] = _bf(pltpu.roll(lo_part, A_HEAD_DIM, 1))
        dst[:, 2 * LANES:3 * LANES] = _bf(pltpu.roll(hi_part, A_HEAD_DIM, 1))
        dst[:, 3 * LANES:4 * LANES] = _bf(hi_part)

    gz = z[:, O_AG:O_AG + A_WIDTH + B_WIDTH]
    gates_ref[...] = gz * jax.nn.sigmoid(gz)

    cq = z[:, O_CQ:O_CQ + B_Q_LORA]
    cqn = _rms(cq, B_Q_LORA) * cqn_ref[...]
    qf = _dot(_bf(cqn), wuq_ref[...])
    bscale = (B_QK_DIM ** -0.5) * LOG2E
    bqn = bqn_ref[...]
    for hd in range(B_HEADS):
        sl = slice(hd * LANES, (hd + 1) * LANES)
        qn = _rms(qf[:, sl], B_QK_DIM) * bqn
        qb_ref[:, sl] = _bf(_rope_b(qn, cosb, sinb) * bscale)

    ckv = z[:, O_CKV:O_CKV + B_KV_LORA]
    ckvn = _rms(ckv, B_KV_LORA) * ckvn_ref[...]
    kv = _dot(_bf(ckvn), wukv_ref[...])
    kr = z[:, O_KR:O_KR + LANES]
    bkn = bkn_ref[...]
    for hd in range(B_HEADS):
        sl = slice(hd * LANES, (hd + 1) * LANES)
        kn = _rms(kv[:, sl] + kr, B_QK_DIM) * bkn
        kb_ref[:, sl] = _bf(_rope_b(kn, cosb, sinb))
    vt = kv[:, B_HEADS * LANES:].T
    ones = jnp.ones((VT_ROWS - B_V_DIM, vt.shape[1]), jnp.bfloat16)
    for hd in range(B_HEADS):
        vt_ref[0, hd * VT_ROWS:hd * VT_ROWS + B_V_DIM, :] = _bf(vt[hd * B_V_DIM:(hd + 1) * B_V_DIM, :])
        vt_ref[0, hd * VT_ROWS + B_V_DIM:(hd + 1) * VT_ROWS, :] = ones


def _proj_call(x2d, B, S, ng, win, tabs, aqn, akn, cqn, ckvn, wuq, wukv, bqn, bkn, g512, g128, tm):
    T = x2d.shape[0]
    nt = S // tm
    cosa, sina, cosb, sinb = tabs
    const = lambda shape: pl.BlockSpec(shape, lambda i: (0,) * len(shape))
    rows = lambda w: pl.BlockSpec((tm, w), lambda i: (i, 0))
    bf = jnp.bfloat16
    out_shape = (
        jax.ShapeDtypeStruct((T, A_WIDTH), bf),
        jax.ShapeDtypeStruct((T, 4 * LANES), bf),
        jax.ShapeDtypeStruct((T, 4 * LANES), bf),
        jax.ShapeDtypeStruct((T, A_WIDTH + B_WIDTH), jnp.float32),
        jax.ShapeDtypeStruct((T, B_HEADS * LANES), bf),
        jax.ShapeDtypeStruct((T, B_HEADS * LANES), bf),
        jax.ShapeDtypeStruct((B, B_HEADS * VT_ROWS, S), bf),
    )
    out_specs = (
        rows(A_WIDTH), rows(4 * LANES), rows(4 * LANES), rows(A_WIDTH + B_WIDTH),
        rows(B_HEADS * LANES), rows(B_HEADS * LANES),
        pl.BlockSpec((1, B_HEADS * VT_ROWS, tm), lambda i: (i // nt, 0, i % nt)),
    )
    in_specs = [
        rows(D_MODEL), const((1, D_MODEL)), const((D_MODEL, N_IN_P)),
        rows(LANES), rows(LANES), rows(LANES), rows(LANES),
        const((1, A_WIDTH)), const((1, LANES)), const((1, B_Q_LORA)), const((1, B_KV_LORA)),
        const((B_Q_LORA, B_HEADS * LANES)), const((B_KV_LORA, B_HEADS * LANES + B_WIDTH)),
        const((1, LANES)), const((1, LANES)),
        const((A_WIDTH, A_WIDTH)), const((LANES, LANES)),
    ]
    return pl.pallas_call(
        _proj_kernel,
        out_shape=out_shape,
        grid=(T // tm,),
        in_specs=in_specs,
        out_specs=out_specs,
        compiler_params=pltpu.CompilerParams(
            dimension_semantics=("arbitrary",), vmem_limit_bytes=VMEM_LIMIT),
        name="proj",
    )(x2d, ng, win, cosa, sina, cosb, sinb, aqn, akn, cqn, ckvn, wuq, wukv, bqn, bkn, g512, g128)


def _attn_a_kernel(sink_ref, q_ref, kp_ref, kc_ref, kn_ref, vp_ref, vc_ref, vn_ref, gate_ref, o_ref, *, nb):
    n = pl.program_id(1)
    qi = lax.broadcasted_iota(jnp.int32, (BLOCK, BLOCK), 0)
    ci = lax.broadcasted_iota(jnp.int32, (BLOCK, BLOCK), 1)
    mask_p = (ci >= qi) & (n > 0)
    mask_n = (ci <= qi) & (n < nb - 1)
    lane = lax.broadcasted_iota(jnp.int32, (BLOCK, LANES), 1)
    low = lane < A_HEAD_DIM
    neg = jnp.float32(-1e30)
    for j in range(A_HEADS // 2):
        q = q_ref[:, j * LANES:(j + 1) * LANES]
        outs = []
        for e in range(2):
            var = 2 * (j // 2) + e
            sl = slice(var * LANES, (var + 1) * LANES)
            sink = sink_ref[2 * j + e] * LOG2E
            s_p = jnp.where(mask_p, _dot_nt(q, kp_ref[:, sl]), neg)
            s_c = _dot_nt(q, kc_ref[:, sl])
            s_n = jnp.where(mask_n, _dot_nt(q, kn_ref[:, sl]), neg)
            m = jnp.maximum(jnp.max(jnp.maximum(jnp.maximum(s_p, s_c), s_n), axis=-1, keepdims=True), sink)
            p_p = jnp.exp2(s_p - m)
            p_c = jnp.exp2(s_c - m)
            p_n = jnp.exp2(s_n - m)
            l = jnp.sum(p_p + p_c + p_n, axis=-1, keepdims=True) + jnp.exp2(sink - m)
            o = _dot(_bf(p_p), vp_ref[:, sl]) + _dot(_bf(p_c), vc_ref[:, sl]) + _dot(_bf(p_n), vn_ref[:, sl])
            outs.append(o * (1.0 / l))
        o_pair = jnp.where(low, outs[0], outs[1])
        o_ref[:, j * LANES:(j + 1) * LANES] = _bf(o_pair * gate_ref[:, j * LANES:(j + 1) * LANES])


def _attn_a_call(qa, ka4, va4, gates, sink, B, S):
    T = qa.shape[0]
    nb = S // BLOCK
    cur = lambda b, n: (b * nb + n, 0)
    prv = lambda b, n: (b * nb + jnp.maximum(n - 1, 0), 0)
    nxt = lambda b, n: (b * nb + jnp.minimum(n + 1, nb - 1), 0)
    w = 4 * LANES
    return pl.pallas_call(
        functools.partial(_attn_a_kernel, nb=nb),
        out_shape=jax.ShapeDtypeStruct((T, A_WIDTH), jnp.bfloat16),
        grid=(B, nb),
        in_specs=[
            pl.BlockSpec(memory_space=pltpu.SMEM),
            pl.BlockSpec((BLOCK, A_WIDTH), cur),
            pl.BlockSpec((BLOCK, w), prv), pl.BlockSpec((BLOCK, w), cur), pl.BlockSpec((BLOCK, w), nxt),
            pl.BlockSpec((BLOCK, w), prv), pl.BlockSpec((BLOCK, w), cur), pl.BlockSpec((BLOCK, w), nxt),
            pl.BlockSpec((BLOCK, A_WIDTH), cur),
        ],
        out_specs=pl.BlockSpec((BLOCK, A_WIDTH), cur),
        compiler_params=pltpu.CompilerParams(
            dimension_semantics=("arbitrary", "arbitrary"), vmem_limit_bytes=VMEM_LIMIT),
        name="attn_a",
    )(sink, qa, ka4, ka4, ka4, va4, va4, va4, gates)


def _attn_b_kernel(q_ref, k_ref, vt_ref, gate_ref, o_ref, s_scr, *, tk):
    S = k_ref.shape[0]
    tq = q_ref.shape[0]
    nk = S // tk
    outs = []
    for hh in range(2):
        q = q_ref[:, hh * LANES:(hh + 1) * LANES]

        def score_step(c, mcol, q=q, hh=hh):
            ks = k_ref[pl.ds(pl.multiple_of(c * tk, tk), tk), hh * LANES:(hh + 1) * LANES]
            st = _dot_nt(ks, q)
            s_scr[pl.ds(pl.multiple_of(c * tk, tk), tk), :] = st
            return jnp.maximum(mcol, jnp.max(st.reshape(tk // 8, 8, tq), axis=0))

        mcol = lax.fori_loop(0, nk, score_step, jnp.full((8, tq), -jnp.inf, jnp.float32))
        m = jnp.max(mcol, axis=0, keepdims=True)

        def pv_step(c, acc, m=m, hh=hh):
            st = s_scr[pl.ds(pl.multiple_of(c * tk, tk), tk), :]
            p = _bf(jnp.exp2(st - m))
            vt = vt_ref[0, hh * VT_ROWS:(hh + 1) * VT_ROWS, pl.ds(pl.multiple_of(c * tk, tk), tk)]
            return acc + _dot(vt, p)

        acc = lax.fori_loop(0, nk, pv_step, jnp.zeros((VT_ROWS, tq), jnp.float32))
        outs.append(acc[:B_V_DIM, :] * (1.0 / acc[B_V_DIM:B_V_DIM + 1, :]))
    o = jnp.concatenate(outs, axis=0).T
    o_ref[...] = _bf(o * gate_ref[...])


def _attn_b_call(qb, kb, vt, gates, B, S, tq, tk):
    T = qb.shape[0]
    nq = S // tq
    npair = B_HEADS // 2
    gate_col0 = A_WIDTH // LANES
    return pl.pallas_call(
        functools.partial(_attn_b_kernel, tk=tk),
        out_shape=jax.ShapeDtypeStruct((T, B_WIDTH), jnp.bfloat16),
        grid=(B, npair, nq),
        in_specs=[
            pl.BlockSpec((tq, 2 * LANES), lambda b, j, i: (b * nq + i, j)),
            pl.BlockSpec((S, 2 * LANES), lambda b, j, i: (b, j)),
            pl.BlockSpec((1, 2 * VT_ROWS, S), lambda b, j, i: (b, j, 0)),
            pl.BlockSpec((tq, LANES), lambda b, j, i: (b * nq + i, gate_col0 + j)),
        ],
        out_specs=pl.BlockSpec((tq, LANES), lambda b, j, i: (b * nq + i, j)),
        scratch_shapes=[pltpu.VMEM((S, tq), jnp.float32)],
        compiler_params=pltpu.CompilerParams(
            dimension_semantics=("arbitrary", "arbitrary", "arbitrary"), vmem_limit_bytes=VMEM_LIMIT),
        name="attn_b",
    )(qb, kb, vt, gates)


def _out_kernel(x_ref, oa_ref, ob_ref, wout_ref, pg_ref, wgate_ref, p_ref, wproj_ref, y_ref):
    mix = jnp.concatenate([oa_ref[...], ob_ref[...]], axis=-1)
    x1 = x_ref[...] + _dot(mix, wout_ref[...])
    hn = _rms(x1, D_MODEL) * pg_ref[...]
    gate = jax.nn.sigmoid(_dot(_bf(hn), wgate_ref[...]))
    y_ref[...] = x1 + gate * _dot(_bf(p_ref[...]), wproj_ref[...])


def _out_call(x2d, oa, ob, wout, pg, wgate, p2d, wproj, tm):
    T = x2d.shape[0]
    const = lambda shape: pl.BlockSpec(shape, lambda i: (0,) * len(shape))
    rows = lambda w: pl.BlockSpec((tm, w), lambda i: (i, 0))
    return pl.pallas_call(
        _out_kernel,
        out_shape=jax.ShapeDtypeStruct((T, D_MODEL), jnp.float32),
        grid=(T // tm,),
        in_specs=[rows(D_MODEL), rows(A_WIDTH), rows(B_WIDTH), const((A_WIDTH + B_WIDTH, D_MODEL)),
                  const((1, D_MODEL)), const((D_MODEL, D_MODEL)), rows(PLE_DIM), const((PLE_DIM, D_MODEL))],
        out_specs=rows(D_MODEL),
        compiler_params=pltpu.CompilerParams(
            dimension_semantics=("arbitrary",), vmem_limit_bytes=VMEM_LIMIT),
        name="out_proj",
    )(x2d, oa, ob, wout, pg, wgate, p2d, wproj)


def _prep_w_in(w_in):
    aq, ak, av, ag, bcq, bckv, bkr, bg = jnp.split(
        w_in, [512, 640, 768, 1280, 1664, 1920, 1952], axis=-1)
    z64 = jnp.zeros(w_in.shape[:-1] + (B_NOPE,), w_in.dtype)
    z32 = jnp.zeros(w_in.shape[:-1] + (LANES - B_QK_DIM,), w_in.dtype)
    return _bf(jnp.concatenate([aq, ak, av, ag, bg, bcq, bckv, z64, bkr, z32], axis=-1))


def _pad_heads(w, real, heads):
    lead = w.shape[:-1]
    w = w.reshape(lead + (heads, real))
    w = jnp.pad(w, [(0, 0)] * len(lead) + [(0, 0), (0, LANES - real)])
    return w.reshape(lead + (heads * LANES,))


def kernel(x, p, positions, norm_g, w_in, a_q_norm, a_k_norm, a_sink, b_cq_norm, b_ckv_norm,
           b_w_uq, b_w_ukv, b_q_norm, b_k_norm, w_out, ple_g, ple_w_gate, ple_w_proj):
    B, S, _ = x.shape
    depth = w_in.shape[0]
    T = B * S
    f32 = jnp.float32

    tabs = _rope_tables(positions)

    win_p = _prep_w_in(w_in)
    wuq_p = _bf(_pad_heads(b_w_uq, B_QK_DIM, B_HEADS))
    ukv = b_w_ukv.reshape(depth, B_KV_LORA, B_HEADS, B_NOPE + B_V_DIM)
    wukv_p = _bf(jnp.concatenate([
        _pad_heads(ukv[..., :B_NOPE].reshape(depth, B_KV_LORA, B_HEADS * B_NOPE), B_NOPE, B_HEADS),
        ukv[..., B_NOPE:].reshape(depth, B_KV_LORA, B_WIDTH)], axis=-1))
    aqn = jnp.tile(a_q_norm, (1, A_HEADS))[:, None, :]
    akn = jnp.tile(a_k_norm, (1, 2))[:, None, :]
    bqn = jnp.pad(b_q_norm, ((0, 0), (0, LANES - B_QK_DIM)))[:, None, :]
    bkn = jnp.pad(b_k_norm, ((0, 0), (0, LANES - B_QK_DIM)))[:, None, :]
    lane = jnp.arange(A_WIDTH)
    g512 = _bf((lane[:, None] // A_HEAD_DIM) == (lane[None, :] // A_HEAD_DIM))
    g128 = g512[:LANES, :LANES]
    wout_b, wgate_b, wproj_b = _bf(w_out), _bf(ple_w_gate), _bf(ple_w_proj)

    xc = x.reshape(T, D_MODEL)
    p2 = p.reshape(depth, T, PLE_DIM)
    for i in range(depth):
        qa, ka4, va4, gates, qb, kb, vt = _proj_call(
            xc, B, S, norm_g[i][None, :], win_p[i], tabs, aqn[i], akn[i],
            b_cq_norm[i][None, :], b_ckv_norm[i][None, :], wuq_p[i], wukv_p[i], bqn[i], bkn[i],
            g512, g128, tm=512)
        oa = _attn_a_call(qa, ka4, va4, gates, a_sink[i].astype(f32), B, S)
        ob = _attn_b_call(qb, kb, vt, gates, B, S, tq=256, tk=512)
        xc = _out_call(xc, oa, ob, wout_b[i], ple_g[i][None, :], wgate_b[i], p2[i], wproj_b[i], tm=512)
    return xc.reshape(B, S, D_MODEL)
```

```python
import functools
import math

import jax
import jax.numpy as jnp
import numpy as np
from jax import lax
from jax.experimental import pallas as pl
from jax.experimental.pallas import tpu as pltpu

D_MODEL = 1024
PLE_DIM = 256
ROPE_THETA = 10000.0
NORM_EPS = 1e-6
BLOCK = 128
A_HEADS = 8
A_HEAD_DIM = 64
A_WIDTH = 512
B_HEADS = 8
B_Q_LORA = 384
B_KV_LORA = 256
B_NOPE = 64
B_ROPE = 32
B_QK_DIM = 96
B_V_DIM = 64
B_WIDTH = 512
LANES = 128
HALF = LANES // 2
LOG2E = math.log2(math.e)
VT_ROWS = 80
VMEM_LIMIT = 56 * 1024 * 1024

_L = np.arange(LANES)
A_SRC = ((_L // 32) % 2) * A_HEAD_DIM + (_L // HALF) * 32 + (_L % 32)
A_GAIN = (_L // HALF) * 32 + (_L % 32)
A_FREQ = _L % 32
A_SIGN = np.where(_L < HALF, -1.0, 1.0).astype(np.float32)
B_SRC = np.full(LANES, -1)
B_SRC[0:16] = B_NOPE + np.arange(16)
B_SRC[16:64] = np.arange(48)
B_SRC[64:80] = B_NOPE + 16 + np.arange(16)
B_SRC[80:96] = 48 + np.arange(16)
B_ROPE_LANE = (B_SRC >= B_NOPE)
B_FREQ = np.where(B_ROPE_LANE, (B_SRC - B_NOPE) % (B_ROPE // 2), 0)
B_SIGN = np.where(B_ROPE_LANE, np.where(B_SRC - B_NOPE < B_ROPE // 2, -1.0, 1.0), 0.0).astype(np.float32)


def _bf(x):
    return x.astype(jnp.bfloat16)


def _dot(a, b):
    return jnp.dot(a, b, preferred_element_type=jnp.float32)


def _dot_nt(a, b):
    return lax.dot_general(a, b, (((1,), (1,)), ((), ())), preferred_element_type=jnp.float32)


def _take(w, src):
    cols = jnp.take(w, jnp.asarray(np.maximum(src, 0)), axis=-1)
    return jnp.where(jnp.asarray(src >= 0), cols, 0.0)


def _table_kernel(pos_ref, inva_ref, sga_ref, invb_ref, sgb_ref, cosa_ref, sina_ref, cosb_ref, sinb_ref):
    pos = pos_ref[...].astype(jnp.float32)
    anga = pos * inva_ref[...]
    cosa_ref[...] = jnp.cos(anga)
    sina_ref[...] = jnp.sin(anga) * sga_ref[...]
    angb = pos * invb_ref[...]
    cosb_ref[...] = jnp.cos(angb)
    sinb_ref[...] = jnp.sin(angb) * sgb_ref[...]


def _rope_tables(positions):
    T = positions.size
    pos = positions.reshape(T, 1)
    inv_a = ROPE_THETA ** (-jnp.arange(A_HEAD_DIM // 2, dtype=jnp.float32) * 2.0 / A_HEAD_DIM)
    inv_b = ROPE_THETA ** (-jnp.arange(B_ROPE // 2, dtype=jnp.float32) * 2.0 / B_ROPE)
    inva_row = inv_a[jnp.asarray(A_FREQ)][None, :]
    invb_row = jnp.where(jnp.asarray(B_ROPE_LANE), inv_b[jnp.asarray(B_FREQ)], 0.0)[None, :]
    sga_row = jnp.asarray(A_SIGN)[None, :]
    sgb_row = jnp.asarray(B_SIGN)[None, :]
    tt = 1024
    row = pl.BlockSpec((1, LANES), lambda i: (0, 0))
    tab = pl.BlockSpec((tt, LANES), lambda i: (i, 0))
    shp = jax.ShapeDtypeStruct((T, LANES), jnp.float32)
    return pl.pallas_call(
        _table_kernel,
        out_shape=(shp, shp, shp, shp),
        grid=(T // tt,),
        in_specs=[pl.BlockSpec((tt, 1), lambda i: (i, 0)), row, row, row, row],
        out_specs=(tab, tab, tab, tab),
        name="rope_tables",
    )(pos, inva_row, sga_row, invb_row, sgb_row)


def _group_sumsq(x, g_ref):
    x2 = x * x
    hi = _bf(x2)
    lo = _bf(x2 - hi.astype(jnp.float32))
    g = g_ref[...]
    return _dot(hi, g) + _dot(lo, g)


def _swap_halves(x):
    return pltpu.roll(x, HALF, 1)


def _sumsq(x):
    return jnp.sum(x * x, axis=-1, keepdims=True)


def _rms(x, n):
    return x * lax.rsqrt(_sumsq(x) * (1.0 / n) + NORM_EPS)


R_AQ, R_AQ_ROT, R_AK, R_AK_ROT, R_BQ, R_BQ_ROT, R_BK, R_BK_ROT = range(8)


def _proj_kernel(x_ref, ng_ref, wc_ref, wa_ref, wg_ref, cosa_ref, sina_ref, cosb_ref, sinb_ref, rows_ref,
                 cqg_ref, ckvg_ref, wuq_ref, wukv_ref, g512_ref, g128_ref,
                 qa_ref, ka4_ref, va4_ref, gates_ref, qb_ref, kb_ref, vt_ref):
    hb = _bf(_rms(x_ref[...], D_MODEL) * ng_ref[...])
    zc = _dot(hb, wc_ref[...])
    cqn = _bf(_rms(zc[:, :B_Q_LORA], B_Q_LORA) * cqg_ref[...])
    ckvn = _bf(_rms(zc[:, B_Q_LORA:B_Q_LORA + B_KV_LORA], B_KV_LORA) * ckvg_ref[...])
    kr = zc[:, B_Q_LORA + B_KV_LORA:]
    qf = _dot(cqn, wuq_ref[...])
    kv = _dot(ckvn, wukv_ref[...])
    za = _dot(hb, wa_ref[...])
    zg = _dot(hb, wg_ref[...])
    gates_ref[...] = zg * jax.nn.sigmoid(zg)

    def row(r):
        return rows_ref[r:r + 1, :]

    cosb, sinb = cosb_ref[...], sinb_ref[...]
    inv_qk = 1.0 / B_QK_DIM

    cq, sq = cosb * row(R_BQ), sinb * row(R_BQ_ROT)
    for hd in range(B_HEADS):
        sl = slice(hd * LANES, (hd + 1) * LANES)
        xq = qf[:, sl]
        r = lax.rsqrt(_sumsq(xq) * inv_qk + NORM_EPS)
        qb_ref[:, sl] = _bf((xq * cq + _swap_halves(xq) * sq) * r)

    krr = kr * (cosb * row(R_BK)) + _swap_halves(kr) * (sinb * row(R_BK_ROT))
    ss_kr = _sumsq(kr)
    gk = row(R_BK)
    for hd in range(B_HEADS):
        sl = slice(hd * LANES, (hd + 1) * LANES)
        kn = kv[:, sl]
        r = lax.rsqrt((_sumsq(kn) + ss_kr) * inv_qk + NORM_EPS)
        kb_ref[:, sl] = _bf((kn * gk + krr) * r)

    vt = kv[:, B_HEADS * LANES:].T
    ones = jnp.ones((VT_ROWS - B_V_DIM, vt.shape[1]), jnp.bfloat16)
    for hd in range(B_HEADS):
        vt_ref[0, hd * VT_ROWS:hd * VT_ROWS + B_V_DIM, :] = _bf(vt[hd * B_V_DIM:(hd + 1) * B_V_DIM, :])
        vt_ref[0, hd * VT_ROWS + B_V_DIM:(hd + 1) * VT_ROWS, :] = ones

    cosa, sina = cosa_ref[...], sina_ref[...]
    inv_hd = 1.0 / A_HEAD_DIM
    aq = za[:, :A_WIDTH]
    rq = lax.rsqrt(_group_sumsq(aq, g512_ref) * inv_hd + NORM_EPS)
    ca, sa = cosa * row(R_AQ), sina * row(R_AQ_ROT)
    for c in range(A_WIDTH // LANES):
        sl = slice(c * LANES, (c + 1) * LANES)
        xs = aq[:, sl]
        qa_ref[:, sl] = _bf((xs * ca + _swap_halves(xs) * sa) * rq[:, sl])

    ak = za[:, A_WIDTH:A_WIDTH + LANES]
    rk = lax.rsqrt(_group_sumsq(ak, g128_ref) * inv_hd + NORM_EPS)
    akr = (ak * (cosa * row(R_AK)) + _swap_halves(ak) * (sina * row(R_AK_ROT))) * rk
    lane = lax.broadcasted_iota(jnp.int32, ak.shape, 1)
    even = ((lane // 32) % 2) == 0
    ke = jnp.where(even, akr, 0.0)
    ko = jnp.where(even, 0.0, akr)
    ka4_ref[:, 0 * LANES:1 * LANES] = _bf(ke)
    ka4_ref[:, 1 * LANES:2 * LANES] = _bf(pltpu.roll(ke, 32, 1))
    ka4_ref[:, 2 * LANES:3 * LANES] = _bf(pltpu.roll(ko, LANES - 32, 1))
    ka4_ref[:, 3 * LANES:4 * LANES] = _bf(ko)
    av = za[:, A_WIDTH + LANES:]
    low = lane < A_HEAD_DIM
    vlo = jnp.where(low, av, 0.0)
    vhi = jnp.where(low, 0.0, av)
    va4_ref[:, 0 * LANES:1 * LANES] = _bf(vlo)
    va4_ref[:, 1 * LANES:2 * LANES] = _bf(_swap_halves(vlo))
    va4_ref[:, 2 * LANES:3 * LANES] = _bf(_swap_halves(vhi))
    va4_ref[:, 3 * LANES:4 * LANES] = _bf(vhi)


def _proj_call(x2d, B, S, ng, wc, wa, wg, tabs, rows, cqg, ckvg, wuq, wukv, g512, g128, tm):
    T = x2d.shape[0]
    nt = S // tm
    cosa, sina, cosb, sinb = tabs
    const = lambda shape: pl.BlockSpec(shape, lambda i: (0,) * len(shape))
    rowblk = lambda w: pl.BlockSpec((tm, w), lambda i: (i, 0))
    bf = jnp.bfloat16
    out_shape = (
        jax.ShapeDtypeStruct((T, A_WIDTH), bf),
        jax.ShapeDtypeStruct((T, 4 * LANES), bf),
        jax.ShapeDtypeStruct((T, 4 * LANES), bf),
        jax.ShapeDtypeStruct((T, A_WIDTH + B_WIDTH), jnp.float32),
        jax.ShapeDtypeStruct((T, B_HEADS * LANES), bf),
        jax.ShapeDtypeStruct((T, B_HEADS * LANES), bf),
        jax.ShapeDtypeStruct((B, B_HEADS * VT_ROWS, S), bf),
    )
    out_specs = (
        rowblk(A_WIDTH), rowblk(4 * LANES), rowblk(4 * LANES), rowblk(A_WIDTH + B_WIDTH),
        rowblk(B_HEADS * LANES), rowblk(B_HEADS * LANES),
        pl.BlockSpec((1, B_HEADS * VT_ROWS, tm), lambda i: (i // nt, 0, i % nt)),
    )
    in_specs = [
        rowblk(D_MODEL), const((1, D_MODEL)),
        const(wc.shape), const(wa.shape), const(wg.shape),
        rowblk(LANES), rowblk(LANES), rowblk(LANES), rowblk(LANES),
        const(rows.shape), const((1, B_Q_LORA)), const((1, B_KV_LORA)),
        const(wuq.shape), const(wukv.shape), const(g512.shape), const(g128.shape),
    ]
    return pl.pallas_call(
        _proj_kernel,
        out_shape=out_shape,
        grid=(T // tm,),
        in_specs=in_specs,
        out_specs=out_specs,
        compiler_params=pltpu.CompilerParams(
            dimension_semantics=("arbitrary",), vmem_limit_bytes=VMEM_LIMIT),
        name="proj",
    )(x2d, ng, wc, wa, wg, cosa, sina, cosb, sinb, rows, cqg, ckvg, wuq, wukv, g512, g128)


def _attn_a_kernel(sink_ref, q_ref, kp_ref, kc_ref, kn_ref, vp_ref, vc_ref, vn_ref, gate_ref, o_ref, *, nb):
    n = pl.program_id(1)
    rows = 2 * BLOCK
    qi = lax.broadcasted_iota(jnp.int32, (rows, 3 * BLOCK), 0) % BLOCK
    ci = lax.broadcasted_iota(jnp.int32, (rows, 3 * BLOCK), 1)
    valid = (ci >= qi) & (ci <= qi + 2 * BLOCK)
    valid = valid & ((ci >= BLOCK) | (n > 0)) & ((ci < 2 * BLOCK) | (n < nb - 1))
    row = lax.broadcasted_iota(jnp.int32, (rows, 1), 0)
    lane = lax.broadcasted_iota(jnp.int32, (rows, LANES), 1)
    low = lane < A_HEAD_DIM
    neg = jnp.float32(-1e30)
    for g in range(2):
        q = jnp.concatenate([q_ref[:, (2 * g) * LANES:(2 * g + 1) * LANES],
                             q_ref[:, (2 * g + 1) * LANES:(2 * g + 2) * LANES]], axis=0)
        outs = []
        for e in range(2):
            sl = slice((2 * g + e) * LANES, (2 * g + e + 1) * LANES)
            kcat = jnp.concatenate([kp_ref[:, sl], kc_ref[:, sl], kn_ref[:, sl]], axis=0)
            vcat = jnp.concatenate([vp_ref[:, sl], vc_ref[:, sl], vn_ref[:, sl]], axis=0)
            sink = jnp.where(row < BLOCK, sink_ref[4 * g + e], sink_ref[4 * g + 2 + e]) * LOG2E
            s = jnp.where(valid, _dot_nt(q, kcat), neg)
            m = jnp.maximum(jnp.max(s, axis=-1, keepdims=True), sink)
            p = jnp.exp2(s - m)
            l = jnp.sum(p, axis=-1, keepdims=True) + jnp.exp2(sink - m)
            outs.append(_dot(_bf(p), vcat) * (1.0 / l))
        o = jnp.where(low, outs[0], outs[1])
        for jj in range(2):
            j = 2 * g + jj
            o_ref[:, j * LANES:(j + 1) * LANES] = _bf(
                o[jj * BLOCK:(jj + 1) * BLOCK, :] * gate_ref[:, j * LANES:(j + 1) * LANES])


def _attn_a_call(qa, ka4, va4, gates, sink, B, S):
    T = qa.shape[0]
    nb = S // BLOCK
    cur = lambda b, n: (b * nb + n, 0)
    prv = lambda b, n: (b * nb + jnp.maximum(n - 1, 0), 0)
    nxt = lambda b, n: (b * nb + jnp.minimum(n + 1, nb - 1), 0)
    w = 4 * LANES
    return pl.pallas_call(
        functools.partial(_attn_a_kernel, nb=nb),
        out_shape=jax.ShapeDtypeStruct((T, A_WIDTH), jnp.bfloat16),
        grid=(B, nb),
        in_specs=[
            pl.BlockSpec(memory_space=pltpu.SMEM),
            pl.BlockSpec((BLOCK, A_WIDTH), cur),
            pl.BlockSpec((BLOCK, w), prv), pl.BlockSpec((BLOCK, w), cur), pl.BlockSpec((BLOCK, w), nxt),
            pl.BlockSpec((BLOCK, w), prv), pl.BlockSpec((BLOCK, w), cur), pl.BlockSpec((BLOCK, w), nxt),
            pl.BlockSpec((BLOCK, A_WIDTH), cur),
        ],
        out_specs=pl.BlockSpec((BLOCK, A_WIDTH), cur),
        compiler_params=pltpu.CompilerParams(
            dimension_semantics=("arbitrary", "arbitrary"), vmem_limit_bytes=VMEM_LIMIT),
        name="attn_a",
    )(sink, qa, ka4, ka4, ka4, va4, va4, va4, gates)


def _attn_b_kernel(q_ref, k_ref, vt_ref, gate_ref, o_ref, *, tk, lag):
    S = k_ref.shape[0]
    tq = q_ref.shape[0]
    nk = S // tk
    outs = []
    for hh in range(2):
        qt = _bf(q_ref[:, hh * LANES:(hh + 1) * LANES].astype(jnp.float32).T)
        m_run = jnp.full((1, tq), -1e30, jnp.float32)
        acc = jnp.zeros((VT_ROWS, tq), jnp.float32)
        inflight = []
        for c in range(nk + lag):
            if c < nk:
                ks = k_ref[c * tk:(c + 1) * tk, hh * LANES:(hh + 1) * LANES]
                inflight.append(_dot(ks, qt))
            if c >= lag:
                cc = c - lag
                st = inflight.pop(0)
                m_new = jnp.maximum(m_run, jnp.max(st, axis=0, keepdims=True))
                alpha = jnp.exp2(m_run - m_new)
                p = _bf(jnp.exp2(st - m_new))
                vt = vt_ref[0, hh * VT_ROWS:(hh + 1) * VT_ROWS, cc * tk:(cc + 1) * tk]
                acc = acc * alpha + _dot(vt, p)
                m_run = m_new
        outs.append(acc[:B_V_DIM, :] * (1.0 / acc[B_V_DIM:B_V_DIM + 1, :]))
    o = jnp.concatenate(outs, axis=0).T
    o_ref[...] = _bf(o * gate_ref[...])


def _attn_b_call(qb, kb, vt, gates, B, S, tq, tk, lag):
    T = qb.shape[0]
    nq = S // tq
    npair = B_HEADS // 2
    gate_col0 = A_WIDTH // LANES
    return pl.pallas_call(
        functools.partial(_attn_b_kernel, tk=tk, lag=lag),
        out_shape=jax.ShapeDtypeStruct((T, B_WIDTH), jnp.bfloat16),
        grid=(B, npair, nq),
        in_specs=[
            pl.BlockSpec((tq, 2 * LANES), lambda b, j, i: (b * nq + i, j)),
            pl.BlockSpec((S, 2 * LANES), lambda b, j, i: (b, j)),
            pl.BlockSpec((1, 2 * VT_ROWS, S), lambda b, j, i: (b, j, 0)),
            pl.BlockSpec((tq, LANES), lambda b, j, i: (b * nq + i, gate_col0 + j)),
        ],
        out_specs=pl.BlockSpec((tq, LANES), lambda b, j, i: (b * nq + i, j)),
        compiler_params=pltpu.CompilerParams(
            dimension_semantics=("arbitrary", "arbitrary", "arbitrary"), vmem_limit_bytes=VMEM_LIMIT),
        name="attn_b",
    )(qb, kb, vt, gates)


def _out_kernel(x_ref, oa_ref, ob_ref, wout_ref, pg_ref, wgate_ref, p_ref, wproj_ref, y_ref):
    mix = jnp.concatenate([oa_ref[...], ob_ref[...]], axis=-1)
    x1 = x_ref[...] + _dot(mix, wout_ref[...])
    hn = _rms(x1, D_MODEL) * pg_ref[...]
    gate = jax.nn.sigmoid(_dot(_bf(hn), wgate_ref[...]))
    y_ref[...] = x1 + gate * _dot(_bf(p_ref[...]), wproj_ref[...])


def _out_call(x2d, oa, ob, wout, pg, wgate, p2d, wproj, tm):
    T = x2d.shape[0]
    const = lambda shape: pl.BlockSpec(shape, lambda i: (0,) * len(shape))
    rows = lambda w: pl.BlockSpec((tm, w), lambda i: (i, 0))
    return pl.pallas_call(
        _out_kernel,
        out_shape=jax.ShapeDtypeStruct((T, D_MODEL), jnp.float32),
        grid=(T // tm,),
        in_specs=[rows(D_MODEL), rows(A_WIDTH), rows(B_WIDTH), const((A_WIDTH + B_WIDTH, D_MODEL)),
                  const((1, D_MODEL)), const((D_MODEL, D_MODEL)), rows(PLE_DIM), const((PLE_DIM, D_MODEL))],
        out_specs=rows(D_MODEL),
        compiler_params=pltpu.CompilerParams(
            dimension_semantics=("arbitrary",), vmem_limit_bytes=VMEM_LIMIT),
        name="out_proj",
    )(x2d, oa, ob, wout, pg, wgate, p2d, wproj)


def _prep_weights(w_in, b_w_uq, b_w_ukv, a_q_norm, a_k_norm, b_q_norm, b_k_norm):
    depth = w_in.shape[0]
    aq, ak, av, ag, bcq, bckv, bkr, bg = jnp.split(w_in, [512, 640, 768, 1280, 1664, 1920, 1952], axis=-1)
    aq_src = np.concatenate([j * LANES + A_SRC for j in range(A_WIDTH // LANES)])
    kr_src = np.where(B_ROPE_LANE, B_SRC - B_NOPE, -1)
    wa = _bf(jnp.concatenate([_take(aq, aq_src), _take(ak, A_SRC), av], axis=-1))
    wc = _bf(jnp.concatenate([bcq, bckv, _take(bkr, kr_src)], axis=-1))
    wg = _bf(jnp.concatenate([ag, bg], axis=-1))
    uq = _take(b_w_uq.reshape(depth, B_Q_LORA, B_HEADS, B_QK_DIM), B_SRC)
    wuq = _bf(uq.reshape(depth, B_Q_LORA, B_HEADS * LANES))
    ukv = b_w_ukv.reshape(depth, B_KV_LORA, B_HEADS, B_NOPE + B_V_DIM)
    kn_src = np.where(B_ROPE_LANE, -1, B_SRC)
    kn = _take(ukv[..., :B_NOPE], kn_src).reshape(depth, B_KV_LORA, B_HEADS * LANES)
    wukv = _bf(jnp.concatenate([kn, ukv[..., B_NOPE:].reshape(depth, B_KV_LORA, B_WIDTH)], axis=-1))

    qscale = (A_HEAD_DIM ** -0.5) * LOG2E
    bscale = (B_QK_DIM ** -0.5) * LOG2E
    gaq = _take(a_q_norm, A_GAIN) * qscale
    gak = _take(a_k_norm, A_GAIN)
    gbq = _take(b_q_norm, B_SRC) * bscale
    gbk = _take(b_k_norm, B_SRC)
    rolled = lambda g: jnp.roll(g, HALF, axis=-1)
    rows = jnp.stack([gaq, rolled(gaq), gak, rolled(gak), gbq, rolled(gbq), gbk, rolled(gbk)], axis=1)
    return wc, wa, wg, wuq, wukv, rows


def kernel(x, p, positions, norm_g, w_in, a_q_norm, a_k_norm, a_sink, b_cq_norm, b_ckv_norm,
           b_w_uq, b_w_ukv, b_q_norm, b_k_norm, w_out, ple_g, ple_w_gate, ple_w_proj):
    B, S, _ = x.shape
    depth = w_in.shape[0]
    T = B * S

    tabs = _rope_tables(positions)
    wc, wa, wg, wuq, wukv, rows = _prep_weights(w_in, b_w_uq, b_w_ukv, a_q_norm, a_k_norm, b_q_norm, b_k_norm)
    lane = np.arange(A_WIDTH)
    same_head = ((lane[:, None] // LANES) == (lane[None, :] // LANES)) & \
                (((lane[:, None] // 32) % 2) == ((lane[None, :] // 32) % 2))
    g512 = jnp.asarray(same_head, jnp.bfloat16)
    g128 = g512[:LANES, :LANES]
    wout_b, wgate_b, wproj_b = _bf(w_out), _bf(ple_w_gate), _bf(ple_w_proj)

    xc = x.reshape(T, D_MODEL)
    p2 = p.reshape(depth, T, PLE_DIM)
    for i in range(depth):
        qa, ka4, va4, gates, qb, kb, vt = _proj_call(
            xc, B, S, norm_g[i][None, :], wc[i], wa[i], wg[i], tabs, rows[i],
            b_cq_norm[i][None, :], b_ckv_norm[i][None, :], wuq[i], wukv[i], g512, g128, tm=512)
        oa = _attn_a_call(qa, ka4, va4, gates, a_sink[i].astype(jnp.float32), B, S)
        ob = _attn_b_call(qb, kb, vt, gates, B, S, tq=512, tk=128, lag=8)
        xc = _out_call(xc, oa, ob, wout_b[i], ple_g[i][None, :], wgate_b[i], p2[i], wproj_b[i], tm=512)
    return xc.reshape(B, S, D_MODEL)
```

```python
import functools
import math

import jax
import jax.numpy as jnp
import numpy as np
from jax import lax
from jax.experimental import pallas as pl
from jax.experimental.pallas import tpu as pltpu

D_MODEL = 1024
PLE_DIM = 256
ROPE_THETA = 10000.0
NORM_EPS = 1e-6
BLOCK = 128
A_HEADS = 8
A_HEAD_DIM = 64
A_WIDTH = 512
B_HEADS = 8
B_Q_LORA = 384
B_KV_LORA = 256
B_NOPE = 64
B_ROPE = 32
B_QK_DIM = 96
B_V_DIM = 64
B_WIDTH = 512
LANES = 128
HALF = LANES // 2
LOG2E = math.log2(math.e)
VT_ROWS = 80
VMEM_LIMIT = 56 * 1024 * 1024

_L = np.arange(LANES)
A_SRC = ((_L // 32) % 2) * A_HEAD_DIM + (_L // HALF) * 32 + (_L % 32)
A_GAIN = (_L // HALF) * 32 + (_L % 32)
A_FREQ = _L % 32
A_SIGN = np.where(_L < HALF, -1.0, 1.0).astype(np.float32)
B_SRC = np.full(LANES, -1)
B_SRC[0:16] = B_NOPE + np.arange(16)
B_SRC[16:64] = np.arange(48)
B_SRC[64:80] = B_NOPE + 16 + np.arange(16)
B_SRC[80:96] = 48 + np.arange(16)
B_ROPE_LANE = (B_SRC >= B_NOPE)
B_FREQ = np.where(B_ROPE_LANE, (B_SRC - B_NOPE) % (B_ROPE // 2), 0)
B_SIGN = np.where(B_ROPE_LANE, np.where(B_SRC - B_NOPE < B_ROPE // 2, -1.0, 1.0), 0.0).astype(np.float32)


def _bf(x):
    return x.astype(jnp.bfloat16)


def _dot(a, b):
    return jnp.dot(a, b, preferred_element_type=jnp.float32)


def _dot_nt(a, b):
    return lax.dot_general(a, b, (((1,), (1,)), ((), ())), preferred_element_type=jnp.float32)


def _take(w, src):
    cols = jnp.take(w, jnp.asarray(np.maximum(src, 0)), axis=-1)
    return jnp.where(jnp.asarray(src >= 0), cols, 0.0)


def _table_kernel(pos_ref, inva_ref, sga_ref, invb_ref, sgb_ref, cosa_ref, sina_ref, cosb_ref, sinb_ref):
    pos = pos_ref[...].astype(jnp.float32)
    anga = pos * inva_ref[...]
    cosa_ref[...] = jnp.cos(anga)
    sina_ref[...] = jnp.sin(anga) * sga_ref[...]
    angb = pos * invb_ref[...]
    cosb_ref[...] = jnp.cos(angb)
    sinb_ref[...] = jnp.sin(angb) * sgb_ref[...]


def _rope_tables(positions):
    T = positions.size
    pos = positions.reshape(T, 1)
    inv_a = ROPE_THETA ** (-jnp.arange(A_HEAD_DIM // 2, dtype=jnp.float32) * 2.0 / A_HEAD_DIM)
    inv_b = ROPE_THETA ** (-jnp.arange(B_ROPE // 2, dtype=jnp.float32) * 2.0 / B_ROPE)
    inva_row = inv_a[jnp.asarray(A_FREQ)][None, :]
    invb_row = jnp.where(jnp.asarray(B_ROPE_LANE), inv_b[jnp.asarray(B_FREQ)], 0.0)[None, :]
    sga_row = jnp.asarray(A_SIGN)[None, :]
    sgb_row = jnp.asarray(B_SIGN)[None, :]
    tt = 1024
    row = pl.BlockSpec((1, LANES), lambda i: (0, 0))
    tab = pl.BlockSpec((tt, LANES), lambda i: (i, 0))
    shp = jax.ShapeDtypeStruct((T, LANES), jnp.float32)
    return pl.pallas_call(
        _table_kernel,
        out_shape=(shp, shp, shp, shp),
        grid=(T // tt,),
        in_specs=[pl.BlockSpec((tt, 1), lambda i: (i, 0)), row, row, row, row],
        out_specs=(tab, tab, tab, tab),
        name="rope_tables",
    )(pos, inva_row, sga_row, invb_row, sgb_row)


def _group_sumsq(x, g_ref):
    x2 = x * x
    hi = _bf(x2)
    lo = _bf(x2 - hi.astype(jnp.float32))
    g = g_ref[...]
    return _dot(hi, g) + _dot(lo, g)


def _swap_halves(x):
    return pltpu.roll(x, HALF, 1)


def _sumsq(x):
    return jnp.sum(x * x, axis=-1, keepdims=True)


def _rms(x, n):
    return x * lax.rsqrt(_sumsq(x) * (1.0 / n) + NORM_EPS)


R_AQ, R_AQ_ROT, R_AK, R_AK_ROT, R_BQ, R_BQ_ROT, R_BK, R_BK_ROT = range(8)


def _proj_kernel(x_ref, ng_ref, wc_ref, wa_ref, wg_ref, cosa_ref, sina_ref, cosb_ref, sinb_ref, rows_ref,
                 cqg_ref, ckvg_ref, wuq_ref, wuqr_ref, wukv_ref, g512_ref, g128_ref,
                 qa_ref, ka4_ref, va4_ref, gates_ref, qb_ref, kb_ref, vt_ref):
    hb = _bf(_rms(x_ref[...], D_MODEL) * ng_ref[...])
    zc = _dot(hb, wc_ref[...])
    za = _dot(hb, wa_ref[...])
    cqn = _bf(_rms(zc[:, :B_Q_LORA], B_Q_LORA) * cqg_ref[...])
    ckvn = _bf(_rms(zc[:, B_Q_LORA:B_Q_LORA + B_KV_LORA], B_KV_LORA) * ckvg_ref[...])
    kr = zc[:, B_Q_LORA + B_KV_LORA:]
    qf = _dot(cqn, wuq_ref[...])
    qrot = _dot(cqn, wuqr_ref[...])
    kv = _dot(ckvn, wukv_ref[...])
    aq = za[:, :A_WIDTH]
    ak = za[:, A_WIDTH:A_WIDTH + LANES]
    ssq_a = _group_sumsq(aq, g512_ref)
    ssk_a = _group_sumsq(ak, g128_ref)
    zg = _dot(hb, wg_ref[...])

    def row(r):
        return rows_ref[r:r + 1, :]

    cosb, sinb = cosb_ref[...], sinb_ref[...]
    inv_qk = 1.0 / B_QK_DIM

    cq, sq = cosb * row(R_BQ), sinb * row(R_BQ_ROT)
    for hd in range(B_HEADS):
        sl = slice(hd * LANES, (hd + 1) * LANES)
        xq = qf[:, sl]
        r = lax.rsqrt(_sumsq(xq) * inv_qk + NORM_EPS)
        qb_ref[:, sl] = _bf((xq * cq + qrot[:, sl] * sq) * r)

    krr = kr * (cosb * row(R_BK)) + _swap_halves(kr) * (sinb * row(R_BK_ROT))
    ss_kr = _sumsq(kr)
    gk = row(R_BK)
    for hd in range(B_HEADS):
        sl = slice(hd * LANES, (hd + 1) * LANES)
        kn = kv[:, sl]
        r = lax.rsqrt((_sumsq(kn) + ss_kr) * inv_qk + NORM_EPS)
        kb_ref[:, sl] = _bf((kn * gk + krr) * r)

    vt = kv[:, B_HEADS * LANES:].T
    ones = jnp.ones((VT_ROWS - B_V_DIM, vt.shape[1]), jnp.bfloat16)
    for hd in range(B_HEADS):
        vt_ref[0, hd * VT_ROWS:hd * VT_ROWS + B_V_DIM, :] = _bf(vt[hd * B_V_DIM:(hd + 1) * B_V_DIM, :])
        vt_ref[0, hd * VT_ROWS + B_V_DIM:(hd + 1) * VT_ROWS, :] = ones

    cosa, sina = cosa_ref[...], sina_ref[...]
    inv_hd = 1.0 / A_HEAD_DIM
    rq = lax.rsqrt(ssq_a * inv_hd + NORM_EPS)
    ca, sa = cosa * row(R_AQ), sina * row(R_AQ_ROT)
    for c in range(A_WIDTH // LANES):
        sl = slice(c * LANES, (c + 1) * LANES)
        xs = aq[:, sl]
        qa_ref[:, sl] = _bf((xs * ca + _swap_halves(xs) * sa) * rq[:, sl])

    rk = lax.rsqrt(ssk_a * inv_hd + NORM_EPS)
    akr = (ak * (cosa * row(R_AK)) + _swap_halves(ak) * (sina * row(R_AK_ROT))) * rk
    lane = lax.broadcasted_iota(jnp.int32, ak.shape, 1)
    even = ((lane // 32) % 2) == 0
    ke = jnp.where(even, akr, 0.0)
    ko = jnp.where(even, 0.0, akr)
    ka4_ref[:, 0 * LANES:1 * LANES] = _bf(ke)
    ka4_ref[:, 1 * LANES:2 * LANES] = _bf(pltpu.roll(ke, 32, 1))
    ka4_ref[:, 2 * LANES:3 * LANES] = _bf(pltpu.roll(ko, LANES - 32, 1))
    ka4_ref[:, 3 * LANES:4 * LANES] = _bf(ko)
    av = za[:, A_WIDTH + LANES:]
    low = lane < A_HEAD_DIM
    vlo = jnp.where(low, av, 0.0)
    vhi = jnp.where(low, 0.0, av)
    va4_ref[:, 0 * LANES:1 * LANES] = _bf(vlo)
    va4_ref[:, 1 * LANES:2 * LANES] = _bf(_swap_halves(vlo))
    va4_ref[:, 2 * LANES:3 * LANES] = _bf(_swap_halves(vhi))
    va4_ref[:, 3 * LANES:4 * LANES] = _bf(vhi)

    gates_ref[...] = zg * jax.nn.sigmoid(zg)


def _layer_spec(stacked, layer):
    tail = stacked.shape[1:]
    return pl.BlockSpec((None,) + tail, lambda *_: (layer,) + (0,) * len(tail))


def _proj_call(x2d, B, S, layer, ng, wc, wa, wg, tabs, rows, cqg, ckvg, wuq, wuqr, wukv, g512, g128, tm):
    T = x2d.shape[0]
    nt = S // tm
    cosa, sina, cosb, sinb = tabs
    const = lambda shape: pl.BlockSpec(shape, lambda i: (0,) * len(shape))
    per_layer = lambda w: _layer_spec(w, layer)
    rowblk = lambda w: pl.BlockSpec((tm, w), lambda i: (i, 0))
    bf = jnp.bfloat16
    out_shape = (
        jax.ShapeDtypeStruct((T, A_WIDTH), bf),
        jax.ShapeDtypeStruct((T, 4 * LANES), bf),
        jax.ShapeDtypeStruct((T, 4 * LANES), bf),
        jax.ShapeDtypeStruct((T, A_WIDTH + B_WIDTH), jnp.float32),
        jax.ShapeDtypeStruct((T, B_HEADS * LANES), bf),
        jax.ShapeDtypeStruct((T, B_HEADS * LANES), bf),
        jax.ShapeDtypeStruct((B, B_HEADS * VT_ROWS, S), bf),
    )
    out_specs = (
        rowblk(A_WIDTH), rowblk(4 * LANES), rowblk(4 * LANES), rowblk(A_WIDTH + B_WIDTH),
        rowblk(B_HEADS * LANES), rowblk(B_HEADS * LANES),
        pl.BlockSpec((1, B_HEADS * VT_ROWS, tm), lambda i: (i // nt, 0, i % nt)),
    )
    in_specs = [
        rowblk(D_MODEL), per_layer(ng),
        per_layer(wc), per_layer(wa), per_layer(wg),
        rowblk(LANES), rowblk(LANES), rowblk(LANES), rowblk(LANES),
        per_layer(rows), per_layer(cqg), per_layer(ckvg),
        per_layer(wuq), per_layer(wuqr), per_layer(wukv), const(g512.shape), const(g128.shape),
    ]
    return pl.pallas_call(
        _proj_kernel,
        out_shape=out_shape,
        grid=(T // tm,),
        in_specs=in_specs,
        out_specs=out_specs,
        compiler_params=pltpu.CompilerParams(
            dimension_semantics=("arbitrary",), vmem_limit_bytes=VMEM_LIMIT),
        name="proj",
    )(x2d, ng, wc, wa, wg, cosa, sina, cosb, sinb, rows, cqg, ckvg, wuq, wuqr, wukv, g512, g128)


def _attn_a_kernel(sink_ref, q_ref, kp_ref, kc_ref, kn_ref, vp_ref, vc_ref, vn_ref, gate_ref, o_ref, *,
                   nb, layer):
    n = pl.program_id(1)
    rows = 2 * BLOCK
    qi = lax.broadcasted_iota(jnp.int32, (rows, 3 * BLOCK), 0) % BLOCK
    ci = lax.broadcasted_iota(jnp.int32, (rows, 3 * BLOCK), 1)
    valid = (ci >= qi) & (ci <= qi + 2 * BLOCK)
    valid = valid & ((ci >= BLOCK) | (n > 0)) & ((ci < 2 * BLOCK) | (n < nb - 1))
    row = lax.broadcasted_iota(jnp.int32, (rows, 1), 0)
    lane = lax.broadcasted_iota(jnp.int32, (rows, LANES), 1)
    low = lane < A_HEAD_DIM
    neg = jnp.float32(-1e30)
    variants = [(g, e) for g in range(2) for e in range(2)]
    scores = []
    for g, e in variants:
        q = jnp.concatenate([q_ref[:, (2 * g) * LANES:(2 * g + 1) * LANES],
                             q_ref[:, (2 * g + 1) * LANES:(2 * g + 2) * LANES]], axis=0)
        sl = slice((2 * g + e) * LANES, (2 * g + e + 1) * LANES)
        kcat = jnp.concatenate([kp_ref[:, sl], kc_ref[:, sl], kn_ref[:, sl]], axis=0)
        scores.append(_dot_nt(q, kcat))
    outs = []
    for (g, e), s in zip(variants, scores):
        sl = slice((2 * g + e) * LANES, (2 * g + e + 1) * LANES)
        vcat = jnp.concatenate([vp_ref[:, sl], vc_ref[:, sl], vn_ref[:, sl]], axis=0)
        sink = jnp.where(row < BLOCK, sink_ref[layer, 4 * g + e], sink_ref[layer, 4 * g + 2 + e]) * LOG2E
        s = jnp.where(valid, s, neg)
        m = jnp.maximum(jnp.max(s, axis=-1, keepdims=True), sink)
        p = jnp.exp2(s - m)
        l = jnp.sum(p, axis=-1, keepdims=True) + jnp.exp2(sink - m)
        outs.append(_dot(_bf(p), vcat) * (1.0 / l))
    for g in range(2):
        o = jnp.where(low, outs[2 * g], outs[2 * g + 1])
        for jj in range(2):
            j = 2 * g + jj
            o_ref[:, j * LANES:(j + 1) * LANES] = _bf(
                o[jj * BLOCK:(jj + 1) * BLOCK, :] * gate_ref[:, j * LANES:(j + 1) * LANES])


def _attn_a_call(qa, ka4, va4, gates, sink, layer, B, S):
    T = qa.shape[0]
    nb = S // BLOCK
    cur = lambda b, n: (b * nb + n, 0)
    prv = lambda b, n: (b * nb + jnp.maximum(n - 1, 0), 0)
    nxt = lambda b, n: (b * nb + jnp.minimum(n + 1, nb - 1), 0)
    w = 4 * LANES
    return pl.pallas_call(
        functools.partial(_attn_a_kernel, nb=nb, layer=layer),
        out_shape=jax.ShapeDtypeStruct((T, A_WIDTH), jnp.bfloat16),
        grid=(B, nb),
        in_specs=[
            pl.BlockSpec(memory_space=pltpu.SMEM),
            pl.BlockSpec((BLOCK, A_WIDTH), cur),
            pl.BlockSpec((BLOCK, w), prv), pl.BlockSpec((BLOCK, w), cur), pl.BlockSpec((BLOCK, w), nxt),
            pl.BlockSpec((BLOCK, w), prv), pl.BlockSpec((BLOCK, w), cur), pl.BlockSpec((BLOCK, w), nxt),
            pl.BlockSpec((BLOCK, A_WIDTH), cur),
        ],
        out_specs=pl.BlockSpec((BLOCK, A_WIDTH), cur),
        compiler_params=pltpu.CompilerParams(
            dimension_semantics=("arbitrary", "arbitrary"), vmem_limit_bytes=VMEM_LIMIT),
        name="attn_a",
    )(sink, qa, ka4, ka4, ka4, va4, va4, va4, gates)


def _attn_b_kernel(q_ref, k_ref, vt_ref, gate_ref, o_ref, *, tk, lag):
    S = k_ref.shape[0]
    tq = q_ref.shape[0]
    nk = S // tk
    outs = []
    for hh in range(2):
        qt = _bf(q_ref[:, hh * LANES:(hh + 1) * LANES].astype(jnp.float32).T)
        m_run = jnp.full((1, tq), -1e30, jnp.float32)
        acc = jnp.zeros((VT_ROWS, tq), jnp.float32)
        inflight = []
        for c in range(nk + lag):
            if c < nk:
                ks = k_ref[c * tk:(c + 1) * tk, hh * LANES:(hh + 1) * LANES]
                inflight.append(_dot(ks, qt))
            if c >= lag:
                cc = c - lag
                st = inflight.pop(0)
                m_new = jnp.maximum(m_run, jnp.max(st, axis=0, keepdims=True))
                alpha = jnp.exp2(m_run - m_new)
                p = _bf(jnp.exp2(st - m_new))
                vt = vt_ref[0, hh * VT_ROWS:(hh + 1) * VT_ROWS, cc * tk:(cc + 1) * tk]
                acc = acc * alpha + _dot(vt, p)
                m_run = m_new
        outs.append(acc[:B_V_DIM, :] * (1.0 / acc[B_V_DIM:B_V_DIM + 1, :]))
    o = jnp.concatenate(outs, axis=0).T
    o_ref[...] = _bf(o * gate_ref[...])


def _attn_b_call(qb, kb, vt, gates, B, S, tq, tk, lag):
    T = qb.shape[0]
    nq = S // tq
    npair = B_HEADS // 2
    gate_col0 = A_WIDTH // LANES
    return pl.pallas_call(
        functools.partial(_attn_b_kernel, tk=tk, lag=lag),
        out_shape=jax.ShapeDtypeStruct((T, B_WIDTH), jnp.bfloat16),
        grid=(B, npair, nq),
        in_specs=[
            pl.BlockSpec((tq, 2 * LANES), lambda b, j, i: (b * nq + i, j)),
            pl.BlockSpec((S, 2 * LANES), lambda b, j, i: (b, j)),
            pl.BlockSpec((1, 2 * VT_ROWS, S), lambda b, j, i: (b, j, 0)),
            pl.BlockSpec((tq, LANES), lambda b, j, i: (b * nq + i, gate_col0 + j)),
        ],
        out_specs=pl.BlockSpec((tq, LANES), lambda b, j, i: (b * nq + i, j)),
        compiler_params=pltpu.CompilerParams(
            dimension_semantics=("arbitrary", "arbitrary", "arbitrary"), vmem_limit_bytes=VMEM_LIMIT),
        name="attn_b",
    )(qb, kb, vt, gates)


def _out_kernel(x_ref, oa_ref, ob_ref, wout_ref, pg_ref, wgate_ref, p_ref, wproj_ref, y_ref):
    mix = jnp.concatenate([oa_ref[...], ob_ref[...]], axis=-1)
    x1 = x_ref[...] + _dot(mix, wout_ref[...])
    hn = _rms(x1, D_MODEL) * pg_ref[...]
    gate = jax.nn.sigmoid(_dot(_bf(hn), wgate_ref[...]))
    y_ref[...] = x1 + gate * _dot(_bf(p_ref[...]), wproj_ref[...])


def _out_call(x2d, oa, ob, layer, wout, pg, wgate, p3d, wproj, tm):
    T = x2d.shape[0]
    per_layer = lambda w: _layer_spec(w, layer)
    rows = lambda w: pl.BlockSpec((tm, w), lambda i: (i, 0))
    return pl.pallas_call(
        _out_kernel,
        out_shape=jax.ShapeDtypeStruct((T, D_MODEL), jnp.float32),
        grid=(T // tm,),
        in_specs=[rows(D_MODEL), rows(A_WIDTH), rows(B_WIDTH), per_layer(wout),
                  per_layer(pg), per_layer(wgate),
                  pl.BlockSpec((None, tm, PLE_DIM), lambda i: (layer, i, 0)), per_layer(wproj)],
        out_specs=rows(D_MODEL),
        compiler_params=pltpu.CompilerParams(
            dimension_semantics=("arbitrary",), vmem_limit_bytes=VMEM_LIMIT),
        name="out_proj",
    )(x2d, oa, ob, wout, pg, wgate, p3d, wproj)


def _prep_weights(w_in, b_w_uq, b_w_ukv, a_q_norm, a_k_norm, b_q_norm, b_k_norm):
    depth = w_in.shape[0]
    aq, ak, av, ag, bcq, bckv, bkr, bg = jnp.split(w_in, [512, 640, 768, 1280, 1664, 1920, 1952], axis=-1)
    aq_src = np.concatenate([j * LANES + A_SRC for j in range(A_WIDTH // LANES)])
    kr_src = np.where(B_ROPE_LANE, B_SRC - B_NOPE, -1)
    wa = _bf(jnp.concatenate([_take(aq, aq_src), _take(ak, A_SRC), av], axis=-1))
    wc = _bf(jnp.concatenate([bcq, bckv, _take(bkr, kr_src)], axis=-1))
    wg = _bf(jnp.concatenate([ag, bg], axis=-1))
    uq = _take(b_w_uq.reshape(depth, B_Q_LORA, B_HEADS, B_QK_DIM), B_SRC)
    wuq = _bf(uq.reshape(depth, B_Q_LORA, B_HEADS * LANES))
    uq_rot = jnp.where(jnp.asarray(B_ROPE_LANE), jnp.roll(uq, HALF, axis=-1), 0.0)
    wuqr = _bf(uq_rot.reshape(depth, B_Q_LORA, B_HEADS * LANES))
    ukv = b_w_ukv.reshape(depth, B_KV_LORA, B_HEADS, B_NOPE + B_V_DIM)
    kn_src = np.where(B_ROPE_LANE, -1, B_SRC)
    kn = _take(ukv[..., :B_NOPE], kn_src).reshape(depth, B_KV_LORA, B_HEADS * LANES)
    wukv = _bf(jnp.concatenate([kn, ukv[..., B_NOPE:].reshape(depth, B_KV_LORA, B_WIDTH)], axis=-1))

    qscale = (A_HEAD_DIM ** -0.5) * LOG2E
    bscale = (B_QK_DIM ** -0.5) * LOG2E
    gaq = _take(a_q_norm, A_GAIN) * qscale
    gak = _take(a_k_norm, A_GAIN)
    gbq = _take(b_q_norm, B_SRC) * bscale
    gbk = _take(b_k_norm, B_SRC)
    rolled = lambda g: jnp.roll(g, HALF, axis=-1)
    rows = jnp.stack([gaq, rolled(gaq), gak, rolled(gak), gbq, rolled(gbq), gbk, rolled(gbk)], axis=1)
    return wc, wa, wg, wuq, wuqr, wukv, rows


def kernel(x, p, positions, norm_g, w_in, a_q_norm, a_k_norm, a_sink, b_cq_norm, b_ckv_norm,
           b_w_uq, b_w_ukv, b_q_norm, b_k_norm, w_out, ple_g, ple_w_gate, ple_w_proj):
    B, S, _ = x.shape
    depth = w_in.shape[0]
    T = B * S

    tabs = _rope_tables(positions)
    wc, wa, wg, wuq, wuqr, wukv, rows = _prep_weights(w_in, b_w_uq, b_w_ukv, a_q_norm, a_k_norm, b_q_norm, b_k_norm)
    lane = np.arange(A_WIDTH)
    same_head = ((lane[:, None] // LANES) == (lane[None, :] // LANES)) & \
                (((lane[:, None] // 32) % 2) == ((lane[None, :] // 32) % 2))
    g512 = jnp.asarray(same_head, jnp.bfloat16)
    g128 = g512[:LANES, :LANES]
    wout_b, wgate_b, wproj_b = _bf(w_out), _bf(ple_w_gate), _bf(ple_w_proj)

    ng, cqg, ckvg, pg = (v[:, None, :] for v in (norm_g, b_cq_norm, b_ckv_norm, ple_g))
    sink = a_sink.astype(jnp.float32)
    xc = x.reshape(T, D_MODEL)
    p3 = p.reshape(depth, T, PLE_DIM)
    for i in range(depth):
        qa, ka4, va4, gates, qb, kb, vt = _proj_call(
            xc, B, S, i, ng, wc, wa, wg, tabs, rows, cqg, ckvg, wuq, wuqr, wukv, g512, g128, tm=512)
        oa = _attn_a_call(qa, ka4, va4, gates, sink, i, B, S)
        ob = _attn_b_call(qb, kb, vt, gates, B, S, tq=512, tk=256, lag=6)
        xc = _out_call(xc, oa, ob, i, wout_b, pg, wgate_b, p3, wproj_b, tm=512)
    return xc.reshape(B, S, D_MODEL)
```

```python
import functools
import math

import jax
import jax.numpy as jnp
import numpy as np
from jax import lax
from jax.experimental import pallas as pl
from jax.experimental.pallas import tpu as pltpu

D_MODEL = 1024
PLE_DIM = 256
ROPE_THETA = 10000.0
NORM_EPS = 1e-6
BLOCK = 128
A_HEADS = 8
A_HEAD_DIM = 64
A_WIDTH = 512
B_HEADS = 8
B_Q_LORA = 384
B_KV_LORA = 256
B_NOPE = 64
B_ROPE = 32
B_QK_DIM = 96
B_V_DIM = 64
B_WIDTH = 512
LANES = 128
HALF = LANES // 2
LOG2E = math.log2(math.e)
VT_ROWS = 80
VMEM_LIMIT = 56 * 1024 * 1024

_L = np.arange(LANES)
A_SRC = ((_L // 32) % 2) * A_HEAD_DIM + (_L // HALF) * 32 + (_L % 32)
A_GAIN = (_L // HALF) * 32 + (_L % 32)
A_FREQ = _L % 32
A_SIGN = np.where(_L < HALF, -1.0, 1.0).astype(np.float32)
B_SRC = np.full(LANES, -1)
B_SRC[0:16] = B_NOPE + np.arange(16)
B_SRC[16:64] = np.arange(48)
B_SRC[64:80] = B_NOPE + 16 + np.arange(16)
B_SRC[80:96] = 48 + np.arange(16)
B_ROPE_LANE = (B_SRC >= B_NOPE)
B_FREQ = np.where(B_ROPE_LANE, (B_SRC - B_NOPE) % (B_ROPE // 2), 0)
B_SIGN = np.where(B_ROPE_LANE, np.where(B_SRC - B_NOPE < B_ROPE // 2, -1.0, 1.0), 0.0).astype(np.float32)


def _bf(x):
    return x.astype(jnp.bfloat16)


def _dot(a, b):
    return jnp.dot(a, b, preferred_element_type=jnp.float32)


def _dot_nt(a, b):
    return lax.dot_general(a, b, (((1,), (1,)), ((), ())), preferred_element_type=jnp.float32)


def _take(w, src):
    cols = jnp.take(w, jnp.asarray(np.maximum(src, 0)), axis=-1)
    return jnp.where(jnp.asarray(src >= 0), cols, 0.0)


def _table_kernel(pos_ref, inva_ref, sga_ref, invb_ref, sgb_ref, cosa_ref, sina_ref, cosb_ref, sinb_ref):
    pos = pos_ref[...].astype(jnp.float32)
    anga = pos * inva_ref[...]
    cosa_ref[...] = jnp.cos(anga)
    sina_ref[...] = jnp.sin(anga) * sga_ref[...]
    angb = pos * invb_ref[...]
    cosb_ref[...] = jnp.cos(angb)
    sinb_ref[...] = jnp.sin(angb) * sgb_ref[...]


def _rope_tables(positions):
    T = positions.size
    pos = positions.reshape(T, 1)
    inv_a = ROPE_THETA ** (-jnp.arange(A_HEAD_DIM // 2, dtype=jnp.float32) * 2.0 / A_HEAD_DIM)
    inv_b = ROPE_THETA ** (-jnp.arange(B_ROPE // 2, dtype=jnp.float32) * 2.0 / B_ROPE)
    inva_row = inv_a[jnp.asarray(A_FREQ)][None, :]
    invb_row = jnp.where(jnp.asarray(B_ROPE_LANE), inv_b[jnp.asarray(B_FREQ)], 0.0)[None, :]
    sga_row = jnp.asarray(A_SIGN)[None, :]
    sgb_row = jnp.asarray(B_SIGN)[None, :]
    tt = 1024
    row = pl.BlockSpec((1, LANES), lambda i: (0, 0))
    tab = pl.BlockSpec((tt, LANES), lambda i: (i, 0))
    shp = jax.ShapeDtypeStruct((T, LANES), jnp.float32)
    return pl.pallas_call(
        _table_kernel,
        out_shape=(shp, shp, shp, shp),
        grid=(T // tt,),
        in_specs=[pl.BlockSpec((tt, 1), lambda i: (i, 0)), row, row, row, row],
        out_specs=(tab, tab, tab, tab),
        name="rope_tables",
    )(pos, inva_row, sga_row, invb_row, sgb_row)


def _group_sumsq(x, g_ref):
    x2 = x * x
    hi = _bf(x2)
    lo = _bf(x2 - hi.astype(jnp.float32))
    g = g_ref[...]
    return _dot(hi, g) + _dot(lo, g)


def _swap_halves(x):
    return pltpu.roll(x, HALF, 1)


def _sumsq(x):
    return jnp.sum(x * x, axis=-1, keepdims=True)


def _rms(x, n):
    return x * lax.rsqrt(_sumsq(x) * (1.0 / n) + NORM_EPS)


R_AQ, R_AQ_ROT, R_AK, R_AK_ROT, R_BQ, R_BQ_ROT, R_BK, R_BK_ROT = range(8)


def _proj_kernel(x_ref, ng_ref, wc_ref, wa_ref, wg_ref, cosa_ref, sina_ref, cosb_ref, sinb_ref, rows_ref,
                 cqg_ref, ckvg_ref, wuq_ref, wuqr_ref, wukv_ref, g256_ref, g128_ref,
                 qa_ref, ka4_ref, va4_ref, gates_ref, qb_ref, kb_ref, vt_ref):
    hb = _bf(_rms(x_ref[...], D_MODEL) * ng_ref[...])
    zc = _dot(hb, wc_ref[...])
    za = _dot(hb, wa_ref[...])
    cqn = _bf(_rms(zc[:, :B_Q_LORA], B_Q_LORA) * cqg_ref[...])
    ckvn = _bf(_rms(zc[:, B_Q_LORA:B_Q_LORA + B_KV_LORA], B_KV_LORA) * ckvg_ref[...])
    kr = zc[:, B_Q_LORA + B_KV_LORA:]
    qf = _dot(cqn, wuq_ref[...])
    qrot = _dot(cqn, wuqr_ref[...])
    kv = _dot(ckvn, wukv_ref[...])
    aq = za[:, :A_WIDTH]
    ak = za[:, A_WIDTH:A_WIDTH + LANES]
    w2 = 2 * LANES
    ssq_a = jnp.concatenate([_group_sumsq(aq[:, :w2], g256_ref), _group_sumsq(aq[:, w2:], g256_ref)], axis=-1)
    ssk_a = _group_sumsq(ak, g128_ref)
    zg = _dot(hb, wg_ref[...])

    def row(r):
        return rows_ref[r:r + 1, :]

    cosb, sinb = cosb_ref[...], sinb_ref[...]
    inv_qk = 1.0 / B_QK_DIM

    cq, sq = cosb * row(R_BQ), sinb * row(R_BQ_ROT)
    for hd in range(B_HEADS):
        sl = slice(hd * LANES, (hd + 1) * LANES)
        xq = qf[:, sl]
        r = lax.rsqrt(_sumsq(xq) * inv_qk + NORM_EPS)
        qb_ref[:, sl] = _bf((xq * cq + qrot[:, sl] * sq) * r)

    krr = kr * (cosb * row(R_BK)) + _swap_halves(kr) * (sinb * row(R_BK_ROT))
    ss_kr = _sumsq(kr)
    gk = row(R_BK)
    for hd in range(B_HEADS):
        sl = slice(hd * LANES, (hd + 1) * LANES)
        kn = kv[:, sl]
        r = lax.rsqrt((_sumsq(kn) + ss_kr) * inv_qk + NORM_EPS)
        kb_ref[:, sl] = _bf((kn * gk + krr) * r)

    vt = kv[:, B_HEADS * LANES:].T
    ones = jnp.ones((VT_ROWS - B_V_DIM, vt.shape[1]), jnp.bfloat16)
    for hd in range(B_HEADS):
        vt_ref[0, hd * VT_ROWS:hd * VT_ROWS + B_V_DIM, :] = _bf(vt[hd * B_V_DIM:(hd + 1) * B_V_DIM, :])
        vt_ref[0, hd * VT_ROWS + B_V_DIM:(hd + 1) * VT_ROWS, :] = ones

    cosa, sina = cosa_ref[...], sina_ref[...]
    inv_hd = 1.0 / A_HEAD_DIM
    rq = lax.rsqrt(ssq_a * inv_hd + NORM_EPS)
    ca, sa = cosa * row(R_AQ), sina * row(R_AQ_ROT)
    for c in range(A_WIDTH // LANES):
        sl = slice(c * LANES, (c + 1) * LANES)
        xs = aq[:, sl]
        qa_ref[:, sl] = _bf((xs * ca + _swap_halves(xs) * sa) * rq[:, sl])

    rk = lax.rsqrt(ssk_a * inv_hd + NORM_EPS)
    akr = (ak * (cosa * row(R_AK)) + _swap_halves(ak) * (sina * row(R_AK_ROT))) * rk
    lane = lax.broadcasted_iota(jnp.int32, ak.shape, 1)
    even = ((lane // 32) % 2) == 0
    ke = jnp.where(even, akr, 0.0)
    ko = jnp.where(even, 0.0, akr)
    ka4_ref[:, 0 * LANES:1 * LANES] = _bf(ke)
    ka4_ref[:, 1 * LANES:2 * LANES] = _bf(pltpu.roll(ke, 32, 1))
    ka4_ref[:, 2 * LANES:3 * LANES] = _bf(pltpu.roll(ko, LANES - 32, 1))
    ka4_ref[:, 3 * LANES:4 * LANES] = _bf(ko)
    av = za[:, A_WIDTH + LANES:]
    low = lane < A_HEAD_DIM
    vlo = jnp.where(low, av, 0.0)
    vhi = jnp.where(low, 0.0, av)
    va4_ref[:, 0 * LANES:1 * LANES] = _bf(vlo)
    va4_ref[:, 1 * LANES:2 * LANES] = _bf(_swap_halves(vlo))
    va4_ref[:, 2 * LANES:3 * LANES] = _bf(_swap_halves(vhi))
    va4_ref[:, 3 * LANES:4 * LANES] = _bf(vhi)

    gates_ref[...] = zg * jax.nn.sigmoid(zg)


def _layer_spec(stacked, layer):
    tail = stacked.shape[1:]
    return pl.BlockSpec((None,) + tail, lambda *_: (layer,) + (0,) * len(tail))


def _proj_call(x2d, B, S, layer, ng, wc, wa, wg, tabs, rows, cqg, ckvg, wuq, wuqr, wukv, g256, g128, tm):
    T = x2d.shape[0]
    nt = S // tm
    cosa, sina, cosb, sinb = tabs
    const = lambda shape: pl.BlockSpec(shape, lambda i: (0,) * len(shape))
    per_layer = lambda w: _layer_spec(w, layer)
    rowblk = lambda w: pl.BlockSpec((tm, w), lambda i: (i, 0))
    bf = jnp.bfloat16
    out_shape = (
        jax.ShapeDtypeStruct((T, A_WIDTH), bf),
        jax.ShapeDtypeStruct((T, 4 * LANES), bf),
        jax.ShapeDtypeStruct((T, 4 * LANES), bf),
        jax.ShapeDtypeStruct((T, A_WIDTH + B_WIDTH), jnp.float32),
        jax.ShapeDtypeStruct((T, B_HEADS * LANES), bf),
        jax.ShapeDtypeStruct((T, B_HEADS * LANES), bf),
        jax.ShapeDtypeStruct((B, B_HEADS * VT_ROWS, S), bf),
    )
    out_specs = (
        rowblk(A_WIDTH), rowblk(4 * LANES), rowblk(4 * LANES), rowblk(A_WIDTH + B_WIDTH),
        rowblk(B_HEADS * LANES), rowblk(B_HEADS * LANES),
        pl.BlockSpec((1, B_HEADS * VT_ROWS, tm), lambda i: (i // nt, 0, i % nt)),
    )
    in_specs = [
        rowblk(D_MODEL), per_layer(ng),
        per_layer(wc), per_layer(wa), per_layer(wg),
        rowblk(LANES), rowblk(LANES), rowblk(LANES), rowblk(LANES),
        per_layer(rows), per_layer(cqg), per_layer(ckvg),
        per_layer(wuq), per_layer(wuqr), per_layer(wukv), const(g256.shape), const(g128.shape),
    ]
    return pl.pallas_call(
        _proj_kernel,
        out_shape=out_shape,
        grid=(T // tm,),
        in_specs=in_specs,
        out_specs=out_specs,
        compiler_params=pltpu.CompilerParams(
            dimension_semantics=("arbitrary",), vmem_limit_bytes=VMEM_LIMIT),
        name="proj",
    )(x2d, ng, wc, wa, wg, cosa, sina, cosb, sinb, rows, cqg, ckvg, wuq, wuqr, wukv, g256, g128)


def _attn_a_kernel(sink_ref, q_ref, kp_ref, kc_ref, kn_ref, vp_ref, vc_ref, vn_ref, gate_ref, o_ref, *,
                   nb, layer):
    n = pl.program_id(1)
    rows = 2 * BLOCK
    qi = lax.broadcasted_iota(jnp.int32, (rows, 3 * BLOCK), 0) % BLOCK
    ci = lax.broadcasted_iota(jnp.int32, (rows, 3 * BLOCK), 1)
    band = (ci >= qi) & (ci <= qi + 2 * BLOCK)
    row = lax.broadcasted_iota(jnp.int32, (rows, 1), 0)
    lane = lax.broadcasted_iota(jnp.int32, (rows, LANES), 1)
    low = lane < A_HEAD_DIM
    neg = jnp.float32(-1e30)
    variants = [(sub, g, e) for sub in range(2) for g in range(2) for e in range(2)]

    def window(prev_ref, cur_ref, next_ref, sub, sl):
        c0, c1 = cur_ref[:BLOCK, sl], cur_ref[BLOCK:, sl]
        parts = [prev_ref[:, sl], c0, c1] if sub == 0 else [c0, c1, next_ref[:, sl]]
        return jnp.concatenate(parts, axis=0)

    scores = []
    for sub, g, e in variants:
        qs = slice(sub * BLOCK, (sub + 1) * BLOCK)
        q = jnp.concatenate([q_ref[qs, (2 * g) * LANES:(2 * g + 1) * LANES],
                             q_ref[qs, (2 * g + 1) * LANES:(2 * g + 2) * LANES]], axis=0)
        sl = slice((2 * g + e) * LANES, (2 * g + e + 1) * LANES)
        scores.append(_dot_nt(q, window(kp_ref, kc_ref, kn_ref, sub, sl)))
    outs = {}
    for (sub, g, e), s in zip(variants, scores):
        blk = 2 * n + sub
        valid = band & ((ci >= BLOCK) | (blk > 0)) & ((ci < 2 * BLOCK) | (blk < nb - 1))
        sl = slice((2 * g + e) * LANES, (2 * g + e + 1) * LANES)
        sink = jnp.where(row < BLOCK, sink_ref[layer, 4 * g + e], sink_ref[layer, 4 * g + 2 + e]) * LOG2E
        s = jnp.where(valid, s, neg)
        s_fold = jnp.maximum(jnp.maximum(s[:, :BLOCK], s[:, BLOCK:2 * BLOCK]), s[:, 2 * BLOCK:])
        m = jnp.maximum(jnp.max(s_fold, axis=-1, keepdims=True), sink)
        p = jnp.exp2(s - m)
        p_fold = p[:, :BLOCK] + p[:, BLOCK:2 * BLOCK] + p[:, 2 * BLOCK:]
        l = jnp.sum(p_fold, axis=-1, keepdims=True) + jnp.exp2(sink - m)
        outs[sub, g, e] = _dot(_bf(p), window(vp_ref, vc_ref, vn_ref, sub, sl)) * (1.0 / l)
    for sub in range(2):
        qs = slice(sub * BLOCK, (sub + 1) * BLOCK)
        for g in range(2):
            o = jnp.where(low, outs[sub, g, 0], outs[sub, g, 1])
            for jj in range(2):
                j = 2 * g + jj
                o_ref[qs, j * LANES:(j + 1) * LANES] = _bf(
                    o[jj * BLOCK:(jj + 1) * BLOCK, :] * gate_ref[qs, j * LANES:(j + 1) * LANES])


def _attn_a_call(qa, ka4, va4, gates, sink, layer, B, S):
    T = qa.shape[0]
    nb = S // BLOCK
    npair = nb // 2
    cur = lambda b, n: (b * npair + n, 0)
    prv = lambda b, n: (b * nb + jnp.maximum(2 * n - 1, 0), 0)
    nxt = lambda b, n: (b * nb + jnp.minimum(2 * n + 2, nb - 1), 0)
    w = 4 * LANES
    return pl.pallas_call(
        functools.partial(_attn_a_kernel, nb=nb, layer=layer),
        out_shape=jax.ShapeDtypeStruct((T, A_WIDTH), jnp.bfloat16),
        grid=(B, npair),
        in_specs=[
            pl.BlockSpec(memory_space=pltpu.SMEM),
            pl.BlockSpec((2 * BLOCK, A_WIDTH), cur),
            pl.BlockSpec((BLOCK, w), prv), pl.BlockSpec((2 * BLOCK, w), cur), pl.BlockSpec((BLOCK, w), nxt),
            pl.BlockSpec((BLOCK, w), prv), pl.BlockSpec((2 * BLOCK, w), cur), pl.BlockSpec((BLOCK, w), nxt),
            pl.BlockSpec((2 * BLOCK, A_WIDTH), cur),
        ],
        out_specs=pl.BlockSpec((2 * BLOCK, A_WIDTH), cur),
        compiler_params=pltpu.CompilerParams(
            dimension_semantics=("arbitrary", "arbitrary"), vmem_limit_bytes=VMEM_LIMIT),
        name="attn_a",
    )(sink, qa, ka4, ka4, ka4, va4, va4, va4, gates)


def _attn_b_kernel(q_ref, k_ref, vt_ref, gate_ref, o_ref, *, tk, lag):
    S = k_ref.shape[0]
    tq = q_ref.shape[0]
    nk = S // tk
    nh = q_ref.shape[1] // LANES
    outs = []
    for hh in range(nh):
        qt = _bf(q_ref[:, hh * LANES:(hh + 1) * LANES].astype(jnp.float32).T)
        m_run = jnp.full((1, tq), -1e30, jnp.float32)
        acc = jnp.zeros((VT_ROWS, tq), jnp.float32)
        inflight = []
        for c in range(nk + lag):
            if c < nk:
                ks = k_ref[c * tk:(c + 1) * tk, hh * LANES:(hh + 1) * LANES]
                inflight.append(_dot(ks, qt))
            if c >= lag:
                cc = c - lag
                st = inflight.pop(0)
                m_new = jnp.maximum(m_run, jnp.max(st, axis=0, keepdims=True))
                alpha = jnp.exp2(m_run - m_new)
                p = _bf(jnp.exp2(st - m_new))
                vt = vt_ref[0, hh * VT_ROWS:(hh + 1) * VT_ROWS, cc * tk:(cc + 1) * tk]
                acc = acc * alpha + _dot(vt, p)
                m_run = m_new
        outs.append(acc[:B_V_DIM, :] * (1.0 / acc[B_V_DIM:B_V_DIM + 1, :]))
    o = jnp.concatenate(outs, axis=0).T
    o_ref[...] = _bf(o * gate_ref[...])


def _attn_b_call(qb, kb, vt, gates, B, S, tq, tk, lag, nh):
    T = qb.shape[0]
    nq = S // tq
    ngrp = B_HEADS // nh
    ow = nh * B_V_DIM
    gate_col0 = A_WIDTH // ow
    return pl.pallas_call(
        functools.partial(_attn_b_kernel, tk=tk, lag=lag),
        out_shape=jax.ShapeDtypeStruct((T, B_WIDTH), jnp.bfloat16),
        grid=(B, ngrp, nq),
        in_specs=[
            pl.BlockSpec((tq, nh * LANES), lambda b, j, i: (b * nq + i, j)),
            pl.BlockSpec((S, nh * LANES), lambda b, j, i: (b, j)),
            pl.BlockSpec((1, nh * VT_ROWS, S), lambda b, j, i: (b, j, 0)),
            pl.BlockSpec((tq, ow), lambda b, j, i: (b * nq + i, gate_col0 + j)),
        ],
        out_specs=pl.BlockSpec((tq, ow), lambda b, j, i: (b * nq + i, j)),
        compiler_params=pltpu.CompilerParams(
            dimension_semantics=("arbitrary", "arbitrary", "arbitrary"), vmem_limit_bytes=VMEM_LIMIT),
        name="attn_b",
    )(qb, kb, vt, gates)


def _out_kernel(x_ref, oa_ref, ob_ref, wout_ref, pg_ref, wgate_ref, p_ref, wproj_ref, y_ref):
    mix = jnp.concatenate([oa_ref[...], ob_ref[...]], axis=-1)
    x1 = x_ref[...] + _dot(mix, wout_ref[...])
    hn = _rms(x1, D_MODEL) * pg_ref[...]
    gate = jax.nn.sigmoid(_dot(_bf(hn), wgate_ref[...]))
    y_ref[...] = x1 + gate * _dot(_bf(p_ref[...]), wproj_ref[...])


def _out_call(x2d, oa, ob, layer, wout, pg, wgate, p3d, wproj, tm):
    T = x2d.shape[0]
    per_layer = lambda w: _layer_spec(w, layer)
    rows = lambda w: pl.BlockSpec((tm, w), lambda i: (i, 0))
    return pl.pallas_call(
        _out_kernel,
        out_shape=jax.ShapeDtypeStruct((T, D_MODEL), jnp.float32),
        grid=(T // tm,),
        in_specs=[rows(D_MODEL), rows(A_WIDTH), rows(B_WIDTH), per_layer(wout),
                  per_layer(pg), per_layer(wgate),
                  pl.BlockSpec((None, tm, PLE_DIM), lambda i: (layer, i, 0)), per_layer(wproj)],
        out_specs=rows(D_MODEL),
        compiler_params=pltpu.CompilerParams(
            dimension_semantics=("arbitrary",), vmem_limit_bytes=VMEM_LIMIT),
        name="out_proj",
    )(x2d, oa, ob, wout, pg, wgate, p3d, wproj)


def _prep_weights(w_in, b_w_uq, b_w_ukv, a_q_norm, a_k_norm, b_q_norm, b_k_norm):
    depth = w_in.shape[0]
    aq, ak, av, ag, bcq, bckv, bkr, bg = jnp.split(w_in, [512, 640, 768, 1280, 1664, 1920, 1952], axis=-1)
    aq_src = np.concatenate([j * LANES + A_SRC for j in range(A_WIDTH // LANES)])
    kr_src = np.where(B_ROPE_LANE, B_SRC - B_NOPE, -1)
    wa = _bf(jnp.concatenate([_take(aq, aq_src), _take(ak, A_SRC), av], axis=-1))
    wc = _bf(jnp.concatenate([bcq, bckv, _take(bkr, kr_src)], axis=-1))
    wg = _bf(jnp.concatenate([ag, bg], axis=-1))
    uq = _take(b_w_uq.reshape(depth, B_Q_LORA, B_HEADS, B_QK_DIM), B_SRC)
    wuq = _bf(uq.reshape(depth, B_Q_LORA, B_HEADS * LANES))
    uq_rot = jnp.where(jnp.asarray(B_ROPE_LANE), jnp.roll(uq, HALF, axis=-1), 0.0)
    wuqr = _bf(uq_rot.reshape(depth, B_Q_LORA, B_HEADS * LANES))
    ukv = b_w_ukv.reshape(depth, B_KV_LORA, B_HEADS, B_NOPE + B_V_DIM)
    kn_src = np.where(B_ROPE_LANE, -1, B_SRC)
    kn = _take(ukv[..., :B_NOPE], kn_src).reshape(depth, B_KV_LORA, B_HEADS * LANES)
    wukv = _bf(jnp.concatenate([kn, ukv[..., B_NOPE:].reshape(depth, B_KV_LORA, B_WIDTH)], axis=-1))

    qscale = (A_HEAD_DIM ** -0.5) * LOG2E
    bscale = (B_QK_DIM ** -0.5) * LOG2E
    gaq = _take(a_q_norm, A_GAIN) * qscale
    gak = _take(a_k_norm, A_GAIN)
    gbq = _take(b_q_norm, B_SRC) * bscale
    gbk = _take(b_k_norm, B_SRC)
    rolled = lambda g: jnp.roll(g, HALF, axis=-1)
    rows = jnp.stack([gaq, rolled(gaq), gak, rolled(gak), gbq, rolled(gbq), gbk, rolled(gbk)], axis=1)
    return wc, wa, wg, wuq, wuqr, wukv, rows


def kernel(x, p, positions, norm_g, w_in, a_q_norm, a_k_norm, a_sink, b_cq_norm, b_ckv_norm,
           b_w_uq, b_w_ukv, b_q_norm, b_k_norm, w_out, ple_g, ple_w_gate, ple_w_proj):
    B, S, _ = x.shape
    depth = w_in.shape[0]
    T = B * S

    tabs = _rope_tables(positions)
    wc, wa, wg, wuq, wuqr, wukv, rows = _prep_weights(w_in, b_w_uq, b_w_ukv, a_q_norm, a_k_norm, b_q_norm, b_k_norm)
    lane = np.arange(2 * LANES)
    same_head = ((lane[:, None] // LANES) == (lane[None, :] // LANES)) & \
                (((lane[:, None] // 32) % 2) == ((lane[None, :] // 32) % 2))
    g256 = jnp.asarray(same_head, jnp.bfloat16)
    g128 = g256[:LANES, :LANES]
    wout_b, wgate_b, wproj_b = _bf(w_out), _bf(ple_w_gate), _bf(ple_w_proj)

    ng, cqg, ckvg, pg = (v[:, None, :] for v in (norm_g, b_cq_norm, b_ckv_norm, ple_g))
    sink = a_sink.astype(jnp.float32)
    xc = x.reshape(T, D_MODEL)
    p3 = p.reshape(depth, T, PLE_DIM)
    for i in range(depth):
        qa, ka4, va4, gates, qb, kb, vt = _proj_call(
            xc, B, S, i, ng, wc, wa, wg, tabs, rows, cqg, ckvg, wuq, wuqr, wukv, g256, g128, tm=512)
        oa = _attn_a_call(qa, ka4, va4, gates, sink, i, B, S)
        ob = _attn_b_call(qb, kb, vt, gates, B, S, tq=512, tk=128, lag=8, nh=2)
        xc = _out_call(xc, oa, ob, i, wout_b, pg, wgate_b, p3, wproj_b, tm=512)
    return xc.reshape(B, S, D_MODEL)
```

```python
import functools
import math

import jax
import jax.numpy as jnp
import numpy as np
from jax import lax
from jax.experimental import pallas as pl
from jax.experimental.pallas import tpu as pltpu

D_MODEL = 1024
PLE_DIM = 256
ROPE_THETA = 10000.0
NORM_EPS = 1e-6
BLOCK = 128
A_HEADS = 8
A_HEAD_DIM = 64
A_WIDTH = 512
B_HEADS = 8
B_Q_LORA = 384
B_KV_LORA = 256
B_NOPE = 64
B_ROPE = 32
B_QK_DIM = 96
B_V_DIM = 64
B_WIDTH = 512
LANES = 128
HALF = LANES // 2
LOG2E = math.log2(math.e)
VT_ROWS = 80
VMEM_LIMIT = 56 * 1024 * 1024

_L = np.arange(LANES)
A_SRC = ((_L // 32) % 2) * A_HEAD_DIM + (_L // HALF) * 32 + (_L % 32)
A_GAIN = (_L // HALF) * 32 + (_L % 32)
A_FREQ = _L % 32
A_SIGN = np.where(_L < HALF, -1.0, 1.0).astype(np.float32)
B_SRC = np.full(LANES, -1)
B_SRC[0:16] = B_NOPE + np.arange(16)
B_SRC[16:64] = np.arange(48)
B_SRC[64:80] = B_NOPE + 16 + np.arange(16)
B_SRC[80:96] = 48 + np.arange(16)
B_ROPE_LANE = (B_SRC >= B_NOPE)
B_FREQ = np.where(B_ROPE_LANE, (B_SRC - B_NOPE) % (B_ROPE // 2), 0)
B_SIGN = np.where(B_ROPE_LANE, np.where(B_SRC - B_NOPE < B_ROPE // 2, -1.0, 1.0), 0.0).astype(np.float32)


def _bf(x):
    return x.astype(jnp.bfloat16)


def _dot(a, b):
    return jnp.dot(a, b, preferred_element_type=jnp.float32)


def _dot_nt(a, b):
    return lax.dot_general(a, b, (((1,), (1,)), ((), ())), preferred_element_type=jnp.float32)


def _take(w, src):
    cols = jnp.take(w, jnp.asarray(np.maximum(src, 0)), axis=-1)
    return jnp.where(jnp.asarray(src >= 0), cols, 0.0)


def _table_kernel(pos_ref, inva_ref, sga_ref, invb_ref, sgb_ref, cosa_ref, sina_ref, cosb_ref, sinb_ref):
    pos = pos_ref[...].astype(jnp.float32)
    anga = pos * inva_ref[...]
    cosa_ref[...] = jnp.cos(anga)
    sina_ref[...] = jnp.sin(anga) * sga_ref[...]
    angb = pos * invb_ref[...]
    cosb_ref[...] = jnp.cos(angb)
    sinb_ref[...] = jnp.sin(angb) * sgb_ref[...]


def _rope_tables(positions):
    T = positions.size
    pos = positions.reshape(T, 1)
    inv_a = ROPE_THETA ** (-jnp.arange(A_HEAD_DIM // 2, dtype=jnp.float32) * 2.0 / A_HEAD_DIM)
    inv_b = ROPE_THETA ** (-jnp.arange(B_ROPE // 2, dtype=jnp.float32) * 2.0 / B_ROPE)
    inva_row = inv_a[jnp.asarray(A_FREQ)][None, :]
    invb_row = jnp.where(jnp.asarray(B_ROPE_LANE), inv_b[jnp.asarray(B_FREQ)], 0.0)[None, :]
    sga_row = jnp.asarray(A_SIGN)[None, :]
    sgb_row = jnp.asarray(B_SIGN)[None, :]
    tt = 1024
    row = pl.BlockSpec((1, LANES), lambda i: (0, 0))
    tab = pl.BlockSpec((tt, LANES), lambda i: (i, 0))
    shp = jax.ShapeDtypeStruct((T, LANES), jnp.float32)
    return pl.pallas_call(
        _table_kernel,
        out_shape=(shp, shp, shp, shp),
        grid=(T // tt,),
        in_specs=[pl.BlockSpec((tt, 1), lambda i: (i, 0)), row, row, row, row],
        out_specs=(tab, tab, tab, tab),
        name="rope_tables",
    )(pos, inva_row, sga_row, invb_row, sgb_row)


def _group_sumsq(x, g_ref):
    x2 = x * x
    hi = _bf(x2)
    lo = _bf(x2 - hi.astype(jnp.float32))
    g = g_ref[...]
    return _dot(hi, g) + _dot(lo, g)


def _swap_halves(x):
    return pltpu.roll(x, HALF, 1)


def _sumsq(x):
    return jnp.sum(x * x, axis=-1, keepdims=True)


def _rms(x, n):
    return x * lax.rsqrt(_sumsq(x) * (1.0 / n) + NORM_EPS)


R_AQ, R_AQ_ROT, R_AK, R_AK_ROT, R_BQ, R_BQ_ROT, R_BK, R_BK_ROT = range(8)


def _proj_kernel(x_ref, ng_ref, wc_ref, wa_ref, wg_ref, cosa_ref, sina_ref, cosb_ref, sinb_ref, rows_ref,
                 cqg_ref, ckvg_ref, wuq_ref, wuqr_ref, wukv_ref, g256_ref, g128_ref,
                 qa_ref, ka4_ref, va4_ref, gates_ref, qb_ref, kb_ref, vt_ref):
    hb = _bf(_rms(x_ref[...], D_MODEL) * ng_ref[...])
    zc = _dot(hb, wc_ref[...])
    za = _dot(hb, wa_ref[...])
    cqn = _bf(_rms(zc[:, :B_Q_LORA], B_Q_LORA) * cqg_ref[...])
    ckvn = _bf(_rms(zc[:, B_Q_LORA:B_Q_LORA + B_KV_LORA], B_KV_LORA) * ckvg_ref[...])
    kr = zc[:, B_Q_LORA + B_KV_LORA:]
    qf = _dot(cqn, wuq_ref[...])
    qrot = _dot(cqn, wuqr_ref[...])
    kv = _dot(ckvn, wukv_ref[...])
    aq = za[:, :A_WIDTH]
    ak = za[:, A_WIDTH:A_WIDTH + LANES]
    w2 = 2 * LANES
    ssq_a = jnp.concatenate([_group_sumsq(aq[:, :w2], g256_ref), _group_sumsq(aq[:, w2:], g256_ref)], axis=-1)
    ssk_a = _group_sumsq(ak, g128_ref)
    zg = _dot(hb, wg_ref[...])

    def row(r):
        return rows_ref[r:r + 1, :]

    cosb, sinb = cosb_ref[...], sinb_ref[...]
    inv_qk = 1.0 / B_QK_DIM

    cq, sq = cosb * row(R_BQ), sinb * row(R_BQ_ROT)
    for hd in range(B_HEADS):
        sl = slice(hd * LANES, (hd + 1) * LANES)
        xq = qf[:, sl]
        r = lax.rsqrt(_sumsq(xq) * inv_qk + NORM_EPS)
        qb_ref[:, sl] = _bf((xq * cq + qrot[:, sl] * sq) * r)

    krr = kr * (cosb * row(R_BK)) + _swap_halves(kr) * (sinb * row(R_BK_ROT))
    ss_kr = _sumsq(kr)
    gk = row(R_BK)
    for hd in range(B_HEADS):
        sl = slice(hd * LANES, (hd + 1) * LANES)
        kn = kv[:, sl]
        r = lax.rsqrt((_sumsq(kn) + ss_kr) * inv_qk + NORM_EPS)
        kb_ref[:, sl] = _bf((kn * gk + krr) * r)

    vt = kv[:, B_HEADS * LANES:].T
    ones = jnp.ones((VT_ROWS - B_V_DIM, vt.shape[1]), jnp.bfloat16)
    for hd in range(B_HEADS):
        vt_ref[0, hd * VT_ROWS:hd * VT_ROWS + B_V_DIM, :] = _bf(vt[hd * B_V_DIM:(hd + 1) * B_V_DIM, :])
        vt_ref[0, hd * VT_ROWS + B_V_DIM:(hd + 1) * VT_ROWS, :] = ones

    cosa, sina = cosa_ref[...], sina_ref[...]
    inv_hd = 1.0 / A_HEAD_DIM
    rq = lax.rsqrt(ssq_a * inv_hd + NORM_EPS)
    ca, sa = cosa * row(R_AQ), sina * row(R_AQ_ROT)
    for c in range(A_WIDTH // LANES):
        sl = slice(c * LANES, (c + 1) * LANES)
        xs = aq[:, sl]
        qa_ref[:, sl] = _bf((xs * ca + _swap_halves(xs) * sa) * rq[:, sl])

    rk = lax.rsqrt(ssk_a * inv_hd + NORM_EPS)
    akr = (ak * (cosa * row(R_AK)) + _swap_halves(ak) * (sina * row(R_AK_ROT))) * rk
    lane = lax.broadcasted_iota(jnp.int32, ak.shape, 1)
    even = ((lane // 32) % 2) == 0
    ke = jnp.where(even, akr, 0.0)
    ko = jnp.where(even, 0.0, akr)
    ka4_ref[:, 0 * LANES:1 * LANES] = _bf(ke)
    ka4_ref[:, 1 * LANES:2 * LANES] = _bf(pltpu.roll(ke, 32, 1))
    ka4_ref[:, 2 * LANES:3 * LANES] = _bf(pltpu.roll(ko, LANES - 32, 1))
    ka4_ref[:, 3 * LANES:4 * LANES] = _bf(ko)
    av = za[:, A_WIDTH + LANES:]
    low = lane < A_HEAD_DIM
    vlo = jnp.where(low, av, 0.0)
    vhi = jnp.where(low, 0.0, av)
    va4_ref[:, 0 * LANES:1 * LANES] = _bf(vlo)
    va4_ref[:, 1 * LANES:2 * LANES] = _bf(_swap_halves(vlo))
    va4_ref[:, 2 * LANES:3 * LANES] = _bf(_swap_halves(vhi))
    va4_ref[:, 3 * LANES:4 * LANES] = _bf(vhi)

    gates_ref[...] = zg * jax.nn.sigmoid(zg)


def _layer_spec(stacked, layer):
    tail = stacked.shape[1:]
    return pl.BlockSpec((None,) + tail, lambda *_: (layer,) + (0,) * len(tail))


def _proj_call(x2d, B, S, layer, ng, wc, wa, wg, tabs, rows, cqg, ckvg, wuq, wuqr, wukv, g256, g128, tm):
    T = x2d.shape[0]
    nt = S // tm
    cosa, sina, cosb, sinb = tabs
    const = lambda shape: pl.BlockSpec(shape, lambda i: (0,) * len(shape))
    per_layer = lambda w: _layer_spec(w, layer)
    rowblk = lambda w: pl.BlockSpec((tm, w), lambda i: (i, 0))
    bf = jnp.bfloat16
    out_shape = (
        jax.ShapeDtypeStruct((T, A_WIDTH), bf),
        jax.ShapeDtypeStruct((T, 4 * LANES), bf),
        jax.ShapeDtypeStruct((T, 4 * LANES), bf),
        jax.ShapeDtypeStruct((T, A_WIDTH + B_WIDTH), jnp.float32),
        jax.ShapeDtypeStruct((T, B_HEADS * LANES), bf),
        jax.ShapeDtypeStruct((T, B_HEADS * LANES), bf),
        jax.ShapeDtypeStruct((B, B_HEADS * VT_ROWS, S), bf),
    )
    out_specs = (
        rowblk(A_WIDTH), rowblk(4 * LANES), rowblk(4 * LANES), rowblk(A_WIDTH + B_WIDTH),
        rowblk(B_HEADS * LANES), rowblk(B_HEADS * LANES),
        pl.BlockSpec((1, B_HEADS * VT_ROWS, tm), lambda i: (i // nt, 0, i % nt)),
    )
    in_specs = [
        rowblk(D_MODEL), per_layer(ng),
        per_layer(wc), per_layer(wa), per_layer(wg),
        rowblk(LANES), rowblk(LANES), rowblk(LANES), rowblk(LANES),
        per_layer(rows), per_layer(cqg), per_layer(ckvg),
        per_layer(wuq), per_layer(wuqr), per_layer(wukv), const(g256.shape), const(g128.shape),
    ]
    return pl.pallas_call(
        _proj_kernel,
        out_shape=out_shape,
        grid=(T // tm,),
        in_specs=in_specs,
        out_specs=out_specs,
        compiler_params=pltpu.CompilerParams(
            dimension_semantics=("arbitrary",), vmem_limit_bytes=VMEM_LIMIT),
        name="proj",
    )(x2d, ng, wc, wa, wg, cosa, sina, cosb, sinb, rows, cqg, ckvg, wuq, wuqr, wukv, g256, g128)


def _attn_a_kernel(sink_ref, q_ref, kp_ref, kc_ref, kn_ref, vp_ref, vc_ref, vn_ref, gate_ref, o_ref, *,
                   nb, layer):
    n = pl.program_id(1)
    nsub = q_ref.shape[0] // BLOCK
    rows = 2 * BLOCK
    qi = lax.broadcasted_iota(jnp.int32, (rows, 3 * BLOCK), 0) % BLOCK
    ci = lax.broadcasted_iota(jnp.int32, (rows, 3 * BLOCK), 1)
    band = (ci >= qi) & (ci <= qi + 2 * BLOCK)
    row = lax.broadcasted_iota(jnp.int32, (rows, 1), 0)
    lane = lax.broadcasted_iota(jnp.int32, (rows, LANES), 1)
    low = lane < A_HEAD_DIM
    neg = jnp.float32(-1e30)
    variants = [(sub, g, e) for sub in range(nsub) for g in range(2) for e in range(2)]

    def window(prev_ref, cur_ref, next_ref, sub, sl):
        blocks = [prev_ref[:, sl]] + [cur_ref[i * BLOCK:(i + 1) * BLOCK, sl] for i in range(nsub)] \
            + [next_ref[:, sl]]
        return jnp.concatenate(blocks[sub:sub + 3], axis=0)

    scores = []
    for sub, g, e in variants:
        qs = slice(sub * BLOCK, (sub + 1) * BLOCK)
        q = jnp.concatenate([q_ref[qs, (2 * g) * LANES:(2 * g + 1) * LANES],
                             q_ref[qs, (2 * g + 1) * LANES:(2 * g + 2) * LANES]], axis=0)
        sl = slice((2 * g + e) * LANES, (2 * g + e + 1) * LANES)
        scores.append(_dot_nt(q, window(kp_ref, kc_ref, kn_ref, sub, sl)))
    outs = {}
    for (sub, g, e), s in zip(variants, scores):
        blk = nsub * n + sub
        valid = band & ((ci >= BLOCK) | (blk > 0)) & ((ci < 2 * BLOCK) | (blk < nb - 1))
        sl = slice((2 * g + e) * LANES, (2 * g + e + 1) * LANES)
        sink = jnp.where(row < BLOCK, sink_ref[layer, 4 * g + e], sink_ref[layer, 4 * g + 2 + e]) * LOG2E
        s = jnp.where(valid, s, neg)
        s_fold = jnp.maximum(jnp.maximum(s[:, :BLOCK], s[:, BLOCK:2 * BLOCK]), s[:, 2 * BLOCK:])
        m = jnp.maximum(jnp.max(s_fold, axis=-1, keepdims=True), sink)
        p = jnp.exp2(s - m)
        p_fold = p[:, :BLOCK] + p[:, BLOCK:2 * BLOCK] + p[:, 2 * BLOCK:]
        l = jnp.sum(p_fold, axis=-1, keepdims=True) + jnp.exp2(sink - m)
        outs[sub, g, e] = _dot(_bf(p), window(vp_ref, vc_ref, vn_ref, sub, sl)) * (1.0 / l)
    for sub in range(nsub):
        qs = slice(sub * BLOCK, (sub + 1) * BLOCK)
        for g in range(2):
            o = jnp.where(low, outs[sub, g, 0], outs[sub, g, 1])
            for jj in range(2):
                j = 2 * g + jj
                o_ref[qs, j * LANES:(j + 1) * LANES] = _bf(
                    o[jj * BLOCK:(jj + 1) * BLOCK, :] * gate_ref[qs, j * LANES:(j + 1) * LANES])


def _attn_a_call(qa, ka4, va4, gates, sink, layer, B, S, nsub):
    T = qa.shape[0]
    nb = S // BLOCK
    nstep = nb // nsub
    tq = nsub * BLOCK
    cur = lambda b, n: (b * nstep + n, 0)
    prv = lambda b, n: (b * nb + jnp.maximum(nsub * n - 1, 0), 0)
    nxt = lambda b, n: (b * nb + jnp.minimum(nsub * n + nsub, nb - 1), 0)
    w = 4 * LANES
    return pl.pallas_call(
        functools.partial(_attn_a_kernel, nb=nb, layer=layer),
        out_shape=jax.ShapeDtypeStruct((T, A_WIDTH), jnp.bfloat16),
        grid=(B, nstep),
        in_specs=[
            pl.BlockSpec(memory_space=pltpu.SMEM),
            pl.BlockSpec((tq, A_WIDTH), cur),
            pl.BlockSpec((BLOCK, w), prv), pl.BlockSpec((tq, w), cur), pl.BlockSpec((BLOCK, w), nxt),
            pl.BlockSpec((BLOCK, w), prv), pl.BlockSpec((tq, w), cur), pl.BlockSpec((BLOCK, w), nxt),
            pl.BlockSpec((tq, A_WIDTH), cur),
        ],
        out_specs=pl.BlockSpec((tq, A_WIDTH), cur),
        compiler_params=pltpu.CompilerParams(
            dimension_semantics=("arbitrary", "arbitrary"), vmem_limit_bytes=VMEM_LIMIT),
        name="attn_a",
    )(sink, qa, ka4, ka4, ka4, va4, va4, va4, gates)


def _attn_b_kernel(q_ref, k_ref, vt_ref, gate_ref, o_ref, *, tk, lag):
    S = k_ref.shape[0]
    tq = q_ref.shape[0]
    nk = S // tk
    nh = q_ref.shape[1] // LANES
    outs = []
    for hh in range(nh):
        qt = _bf(q_ref[:, hh * LANES:(hh + 1) * LANES].astype(jnp.float32).T)
        m_run = jnp.full((1, tq), -1e30, jnp.float32)
        acc = jnp.zeros((VT_ROWS, tq), jnp.float32)
        inflight = []
        for c in range(nk + lag):
            if c < nk:
                ks = k_ref[c * tk:(c + 1) * tk, hh * LANES:(hh + 1) * LANES]
                inflight.append(_dot(ks, qt))
            if c >= lag:
                cc = c - lag
                st = inflight.pop(0)
                m_new = jnp.maximum(m_run, jnp.max(st, axis=0, keepdims=True))
                alpha = jnp.exp2(m_run - m_new)
                p = _bf(jnp.exp2(st - m_new))
                vt = vt_ref[0, hh * VT_ROWS:(hh + 1) * VT_ROWS, cc * tk:(cc + 1) * tk]
                acc = acc * alpha + _dot(vt, p)
                m_run = m_new
        outs.append(acc[:B_V_DIM, :] * (1.0 / acc[B_V_DIM:B_V_DIM + 1, :]))
    o = jnp.concatenate(outs, axis=0).T
    o_ref[...] = _bf(o * gate_ref[...])


def _attn_b_call(qb, kb, vt, gates, B, S, tq, tk, lag, nh):
    T = qb.shape[0]
    nq = S // tq
    ngrp = B_HEADS // nh
    ow = nh * B_V_DIM
    gate_col0 = A_WIDTH // ow
    return pl.pallas_call(
        functools.partial(_attn_b_kernel, tk=tk, lag=lag),
        out_shape=jax.ShapeDtypeStruct((T, B_WIDTH), jnp.bfloat16),
        grid=(B, ngrp, nq),
        in_specs=[
            pl.BlockSpec((tq, nh * LANES), lambda b, j, i: (b * nq + i, j)),
            pl.BlockSpec((S, nh * LANES), lambda b, j, i: (b, j)),
            pl.BlockSpec((1, nh * VT_ROWS, S), lambda b, j, i: (b, j, 0)),
            pl.BlockSpec((tq, ow), lambda b, j, i: (b * nq + i, gate_col0 + j)),
        ],
        out_specs=pl.BlockSpec((tq, ow), lambda b, j, i: (b * nq + i, j)),
        compiler_params=pltpu.CompilerParams(
            dimension_semantics=("arbitrary", "arbitrary", "arbitrary"), vmem_limit_bytes=VMEM_LIMIT),
        name="attn_b",
    )(qb, kb, vt, gates)


def _out_kernel(x_ref, oa_ref, ob_ref, wout_ref, pg_ref, wgate_ref, p_ref, wproj_ref, y_ref):
    mix = jnp.concatenate([oa_ref[...], ob_ref[...]], axis=-1)
    x1 = x_ref[...] + _dot(mix, wout_ref[...])
    hn = _rms(x1, D_MODEL) * pg_ref[...]
    gate = jax.nn.sigmoid(_dot(_bf(hn), wgate_ref[...]))
    y_ref[...] = x1 + gate * _dot(_bf(p_ref[...]), wproj_ref[...])


def _out_call(x2d, oa, ob, layer, wout, pg, wgate, p3d, wproj, tm):
    T = x2d.shape[0]
    per_layer = lambda w: _layer_spec(w, layer)
    rows = lambda w: pl.BlockSpec((tm, w), lambda i: (i, 0))
    return pl.pallas_call(
        _out_kernel,
        out_shape=jax.ShapeDtypeStruct((T, D_MODEL), jnp.float32),
        grid=(T // tm,),
        in_specs=[rows(D_MODEL), rows(A_WIDTH), rows(B_WIDTH), per_layer(wout),
                  per_layer(pg), per_layer(wgate),
                  pl.BlockSpec((None, tm, PLE_DIM), lambda i: (layer, i, 0)), per_layer(wproj)],
        out_specs=rows(D_MODEL),
        compiler_params=pltpu.CompilerParams(
            dimension_semantics=("arbitrary",), vmem_limit_bytes=VMEM_LIMIT),
        name="out_proj",
    )(x2d, oa, ob, wout, pg, wgate, p3d, wproj)


def _prep_weights(w_in, b_w_uq, b_w_ukv, a_q_norm, a_k_norm, b_q_norm, b_k_norm):
    depth = w_in.shape[0]
    aq, ak, av, ag, bcq, bckv, bkr, bg = jnp.split(w_in, [512, 640, 768, 1280, 1664, 1920, 1952], axis=-1)
    aq_src = np.concatenate([j * LANES + A_SRC for j in range(A_WIDTH // LANES)])
    kr_src = np.where(B_ROPE_LANE, B_SRC - B_NOPE, -1)
    wa = _bf(jnp.concatenate([_take(aq, aq_src), _take(ak, A_SRC), av], axis=-1))
    wc = _bf(jnp.concatenate([bcq, bckv, _take(bkr, kr_src)], axis=-1))
    wg = _bf(jnp.concatenate([ag, bg], axis=-1))
    uq = _take(b_w_uq.reshape(depth, B_Q_LORA, B_HEADS, B_QK_DIM), B_SRC)
    wuq = _bf(uq.reshape(depth, B_Q_LORA, B_HEADS * LANES))
    uq_rot = jnp.where(jnp.asarray(B_ROPE_LANE), jnp.roll(uq, HALF, axis=-1), 0.0)
    wuqr = _bf(uq_rot.reshape(depth, B_Q_LORA, B_HEADS * LANES))
    ukv = b_w_ukv.reshape(depth, B_KV_LORA, B_HEADS, B_NOPE + B_V_DIM)
    kn_src = np.where(B_ROPE_LANE, -1, B_SRC)
    kn = _take(ukv[..., :B_NOPE], kn_src).reshape(depth, B_KV_LORA, B_HEADS * LANES)
    wukv = _bf(jnp.concatenate([kn, ukv[..., B_NOPE:].reshape(depth, B_KV_LORA, B_WIDTH)], axis=-1))

    qscale = (A_HEAD_DIM ** -0.5) * LOG2E
    bscale = (B_QK_DIM ** -0.5) * LOG2E
    gaq = _take(a_q_norm, A_GAIN) * qscale
    gak = _take(a_k_norm, A_GAIN)
    gbq = _take(b_q_norm, B_SRC) * bscale
    gbk = _take(b_k_norm, B_SRC)
    rolled = lambda g: jnp.roll(g, HALF, axis=-1)
    rows = jnp.stack([gaq, rolled(gaq), gak, rolled(gak), gbq, rolled(gbq), gbk, rolled(gbk)], axis=1)
    return wc, wa, wg, wuq, wuqr, wukv, rows


def kernel(x, p, positions, norm_g, w_in, a_q_norm, a_k_norm, a_sink, b_cq_norm, b_ckv_norm,
           b_w_uq, b_w_ukv, b_q_norm, b_k_norm, w_out, ple_g, ple_w_gate, ple_w_proj):
    B, S, _ = x.shape
    depth = w_in.shape[0]
    T = B * S

    tabs = _rope_tables(positions)
    wc, wa, wg, wuq, wuqr, wukv, rows = _prep_weights(w_in, b_w_uq, b_w_ukv, a_q_norm, a_k_norm, b_q_norm, b_k_norm)
    lane = np.arange(2 * LANES)
    same_head = ((lane[:, None] // LANES) == (lane[None, :] // LANES)) & \
                (((lane[:, None] // 32) % 2) == ((lane[None, :] // 32) % 2))
    g256 = jnp.asarray(same_head, jnp.bfloat16)
    g128 = g256[:LANES, :LANES]
    wout_b, wgate_b, wproj_b = _bf(w_out), _bf(ple_w_gate), _bf(ple_w_proj)

    ng, cqg, ckvg, pg = (v[:, None, :] for v in (norm_g, b_cq_norm, b_ckv_norm, ple_g))
    sink = a_sink.astype(jnp.float32)
    xc = x.reshape(T, D_MODEL)
    p3 = p.reshape(depth, T, PLE_DIM)
    for i in range(depth):
        qa, ka4, va4, gates, qb, kb, vt = _proj_call(
            xc, B, S, i, ng, wc, wa, wg, tabs, rows, cqg, ckvg, wuq, wuqr, wukv, g256, g128, tm=512)
        oa = _attn_a_call(qa, ka4, va4, gates, sink, i, B, S, nsub=4)
        ob = _attn_b_call(qb, kb, vt, gates, B, S, tq=1024, tk=128, lag=8, nh=2)
        xc = _out_call(xc, oa, ob, i, wout_b, pg, wgate_b, p3, wproj_b, tm=512)
    return xc.reshape(B, S, D_MODEL)
```

```python
import functools
import math

import jax
import jax.numpy as jnp
import numpy as np
from jax import lax
from jax.experimental import pallas as pl
from jax.experimental.pallas import tpu as pltpu

D_MODEL = 1024
PLE_DIM = 256
ROPE_THETA = 10000.0
NORM_EPS = 1e-6
BLOCK = 128
A_HEADS = 8
A_HEAD_DIM = 64
A_WIDTH = 512
B_HEADS = 8
B_Q_LORA = 384
B_KV_LORA = 256
B_NOPE = 64
B_ROPE = 32
B_QK_DIM = 96
B_V_DIM = 64
B_WIDTH = 512
LANES = 128
HALF = LANES // 2
LOG2E = math.log2(math.e)
VT_ROWS = 80
VMEM_LIMIT = 56 * 1024 * 1024

PROJ_ROWS = 512
OUT_ROWS = 1024
A_BLOCKS_PER_STEP = 4
B_QUERIES_PER_STEP = 1024
B_HEADS_PER_STEP = 2
B_KEY_CHUNK = 128
B_SCORE_LEAD = 8

_L = np.arange(LANES)
A_SRC = ((_L // 32) % 2) * A_HEAD_DIM + (_L // HALF) * 32 + (_L % 32)
A_GAIN = (_L // HALF) * 32 + (_L % 32)
A_FREQ = _L % 32
A_SIGN = np.where(_L < HALF, -1.0, 1.0).astype(np.float32)
B_SRC = np.full(LANES, -1)
B_SRC[0:32:2] = B_NOPE + np.arange(16)
B_SRC[HALF:HALF + 32:2] = B_NOPE + 16 + np.arange(16)
_free = [l for l in range(B_QK_DIM) if B_SRC[l] < 0]
B_SRC[_free] = np.arange(B_NOPE)
B_ROPE_LANE = (B_SRC >= B_NOPE)


def _bf(x):
    return x.astype(jnp.bfloat16)


def _dot(a, b):
    return jnp.dot(a, b, preferred_element_type=jnp.float32)


def _dot_nt(a, b):
    return lax.dot_general(a, b, (((1,), (1,)), ((), ())), preferred_element_type=jnp.float32)


def _take(w, src):
    src = np.asarray(src)
    parts, start = [], 0
    for i in range(1, len(src) + 1):
        run_ends = i == len(src) or (src[i] != src[i - 1] + 1 if src[i - 1] >= 0 else src[i] >= 0)
        if run_ends:
            if src[start] < 0:
                parts.append(jnp.zeros(w.shape[:-1] + (i - start,), w.dtype))
            else:
                parts.append(w[..., int(src[start]):int(src[i - 1]) + 1])
            start = i
    return jnp.concatenate(parts, axis=-1)


def _table_kernel(pos_ref, inv_ref, sign_ref, cos_ref, sin_ref):
    ang = pos_ref[...].astype(jnp.float32) * inv_ref[...]
    cos_ref[...] = jnp.cos(ang)
    sin_ref[...] = jnp.sin(ang) * sign_ref[...]


def _rope_tables(positions):
    T = positions.size
    pos = positions.reshape(T, 1)
    inv_a = ROPE_THETA ** (-jnp.arange(A_HEAD_DIM // 2, dtype=jnp.float32) * 2.0 / A_HEAD_DIM)
    inv_row = inv_a[jnp.asarray(A_FREQ)][None, :]
    sign_row = jnp.asarray(A_SIGN)[None, :]
    tt = 1024
    row = pl.BlockSpec((1, LANES), lambda i: (0, 0))
    tab = pl.BlockSpec((tt, LANES), lambda i: (i, 0))
    shp = jax.ShapeDtypeStruct((T, LANES), jnp.float32)
    return pl.pallas_call(
        _table_kernel,
        out_shape=(shp, shp),
        grid=(T // tt,),
        in_specs=[pl.BlockSpec((tt, 1), lambda i: (i, 0)), row, row],
        out_specs=(tab, tab),
        name="rope_tables",
    )(pos, inv_row, sign_row)


def _group_sumsq(x, g_ref):
    x2 = x * x
    hi = _bf(x2)
    lo = _bf(x2 - hi.astype(jnp.float32))
    g = g_ref[...]
    return _dot(hi, g) + _dot(lo, g)


def _swap_halves(x):
    return pltpu.roll(x, HALF, 1)


def _sumsq(x):
    return jnp.sum(x * x, axis=-1, keepdims=True)


def _rms(x, n):
    return x * lax.rsqrt(_sumsq(x) * (1.0 / n) + NORM_EPS)


R_AQ, R_AQ_ROT, R_AK, R_AK_ROT, R_BQ, R_BQ_ROT, R_BK, R_BK_ROT = range(8)


def _proj_kernel(x_ref, ng_ref, wc_ref, wa_ref, wg_ref, cosa_ref, sina_ref, rows_ref,
                 cqg_ref, ckvg_ref, wuq_ref, wuqr_ref, wukv_ref, g256_ref, g128_ref,
                 qa_ref, ka4_ref, va4_ref, gates_ref, qb_ref, kb_ref, vt_ref):
    hb = _bf(_rms(x_ref[...], D_MODEL) * ng_ref[...])
    zc = _dot(hb, wc_ref[...])
    za = _dot(hb, wa_ref[...])
    cqn = _bf(_rms(zc[:, :B_Q_LORA], B_Q_LORA) * cqg_ref[...])
    ckvn = _bf(_rms(zc[:, B_Q_LORA:B_Q_LORA + B_KV_LORA], B_KV_LORA) * ckvg_ref[...])
    kr = zc[:, B_Q_LORA + B_KV_LORA:]
    qf = _dot(cqn, wuq_ref[...])
    qrot = _dot(cqn, wuqr_ref[...])
    kv = _dot(ckvn, wukv_ref[...])
    aq = za[:, :A_WIDTH]
    ak = za[:, A_WIDTH:A_WIDTH + LANES]
    w2 = 2 * LANES
    ssq_a = jnp.concatenate([_group_sumsq(aq[:, :w2], g256_ref), _group_sumsq(aq[:, w2:], g256_ref)], axis=-1)
    ssk_a = _group_sumsq(ak, g128_ref)
    zg = _dot(hb, wg_ref[...])

    def row(r):
        return rows_ref[r:r + 1, :]

    cosa, sina = cosa_ref[...], sina_ref[...]
    lane_t = lax.broadcasted_iota(jnp.int32, cosa.shape, 1)
    cosb = jnp.where((lane_t % 2 == 0) & (lane_t % HALF < B_ROPE), cosa, 1.0)
    sinb = sina
    inv_qk = 1.0 / B_QK_DIM

    cq, sq = cosb * row(R_BQ), sinb * row(R_BQ_ROT)
    for hd in range(B_HEADS):
        sl = slice(hd * LANES, (hd + 1) * LANES)
        xq = qf[:, sl]
        r = lax.rsqrt(_sumsq(xq) * inv_qk + NORM_EPS)
        qb_ref[:, sl] = _bf((xq * cq + qrot[:, sl] * sq) * r)

    krr = kr * (cosb * row(R_BK)) + _swap_halves(kr) * (sinb * row(R_BK_ROT))
    ss_kr = _sumsq(kr)
    gk = row(R_BK)
    for hd in range(B_HEADS):
        sl = slice(hd * LANES, (hd + 1) * LANES)
        kn = kv[:, sl]
        r = lax.rsqrt((_sumsq(kn) + ss_kr) * inv_qk + NORM_EPS)
        kb_ref[:, sl] = _bf((kn * gk + krr) * r)

    vt = kv[:, B_HEADS * LANES:].T
    ones = jnp.ones((VT_ROWS - B_V_DIM, vt.shape[1]), jnp.bfloat16)
    for hd in range(B_HEADS):
        vt_ref[0, hd * VT_ROWS:hd * VT_ROWS + B_V_DIM, :] = _bf(vt[hd * B_V_DIM:(hd + 1) * B_V_DIM, :])
        vt_ref[0, hd * VT_ROWS + B_V_DIM:(hd + 1) * VT_ROWS, :] = ones

    inv_hd = 1.0 / A_HEAD_DIM
    rq = lax.rsqrt(ssq_a * inv_hd + NORM_EPS)
    ca, sa = cosa * row(R_AQ), sina * row(R_AQ_ROT)
    for c in range(A_WIDTH // LANES):
        sl = slice(c * LANES, (c + 1) * LANES)
        xs = aq[:, sl]
        qa_ref[:, sl] = _bf((xs * ca + _swap_halves(xs) * sa) * rq[:, sl])

    rk = lax.rsqrt(ssk_a * inv_hd + NORM_EPS)
    akr = (ak * (cosa * row(R_AK)) + _swap_halves(ak) * (sina * row(R_AK_ROT))) * rk
    lane = lax.broadcasted_iota(jnp.int32, ak.shape, 1)
    even = ((lane // 32) % 2) == 0
    ke = jnp.where(even, akr, 0.0)
    ko = jnp.where(even, 0.0, akr)
    ka4_ref[:, 0 * LANES:1 * LANES] = _bf(ke)
    ka4_ref[:, 1 * LANES:2 * LANES] = _bf(pltpu.roll(ke, 32, 1))
    ka4_ref[:, 2 * LANES:3 * LANES] = _bf(pltpu.roll(ko, LANES - 32, 1))
    ka4_ref[:, 3 * LANES:4 * LANES] = _bf(ko)
    av = za[:, A_WIDTH + LANES:]
    low = lane < A_HEAD_DIM
    vlo = jnp.where(low, av, 0.0)
    vhi = jnp.where(low, 0.0, av)
    va4_ref[:, 0 * LANES:1 * LANES] = _bf(vlo)
    va4_ref[:, 1 * LANES:2 * LANES] = _bf(_swap_halves(vlo))
    va4_ref[:, 2 * LANES:3 * LANES] = _bf(_swap_halves(vhi))
    va4_ref[:, 3 * LANES:4 * LANES] = _bf(vhi)

    gates_ref[...] = zg * jax.nn.sigmoid(zg)


def _layer_spec(stacked, layer):
    tail = stacked.shape[1:]
    return pl.BlockSpec((None,) + tail, lambda *_: (layer,) + (0,) * len(tail))


def _proj_call(x2d, B, S, layer, ng, wc, wa, wg, tabs, rows, cqg, ckvg, wuq, wuqr, wukv, g256, g128, tm):
    T = x2d.shape[0]
    nt = S // tm
    cosa, sina = tabs
    const = lambda shape: pl.BlockSpec(shape, lambda i: (0,) * len(shape))
    per_layer = lambda w: _layer_spec(w, layer)
    rowblk = lambda w: pl.BlockSpec((tm, w), lambda i: (i, 0))
    bf = jnp.bfloat16
    out_shape = (
        jax.ShapeDtypeStruct((T, A_WIDTH), bf),
        jax.ShapeDtypeStruct((T, 4 * LANES), bf),
        jax.ShapeDtypeStruct((T, 4 * LANES), bf),
        jax.ShapeDtypeStruct((T, A_WIDTH + B_WIDTH), jnp.float32),
        jax.ShapeDtypeStruct((T, B_HEADS * LANES), bf),
        jax.ShapeDtypeStruct((T, B_HEADS * LANES), bf),
        jax.ShapeDtypeStruct((B, B_HEADS * VT_ROWS, S), bf),
    )
    out_specs = (
        rowblk(A_WIDTH), rowblk(4 * LANES), rowblk(4 * LANES), rowblk(A_WIDTH + B_WIDTH),
        rowblk(B_HEADS * LANES), rowblk(B_HEADS * LANES),
        pl.BlockSpec((1, B_HEADS * VT_ROWS, tm), lambda i: (i // nt, 0, i % nt)),
    )
    in_specs = [
        rowblk(D_MODEL), per_layer(ng),
        per_layer(wc), per_layer(wa), per_layer(wg),
        rowblk(LANES), rowblk(LANES),
        per_layer(rows), per_layer(cqg), per_layer(ckvg),
        per_layer(wuq), per_layer(wuqr), per_layer(wukv), const(g256.shape), const(g128.shape),
    ]
    return pl.pallas_call(
        _proj_kernel,
        out_shape=out_shape,
        grid=(T // tm,),
        in_specs=in_specs,
        out_specs=out_specs,
        compiler_params=pltpu.CompilerParams(
            dimension_semantics=("arbitrary",), vmem_limit_bytes=VMEM_LIMIT),
        name="proj",
    )(x2d, ng, wc, wa, wg, cosa, sina, rows, cqg, ckvg, wuq, wuqr, wukv, g256, g128)


def _attn_a_kernel(sink_ref, q_ref, kp_ref, kc_ref, kn_ref, vp_ref, vc_ref, vn_ref, gate_ref, o_ref, *,
                   nb, layer):
    n = pl.program_id(1)
    nsub = q_ref.shape[0] // BLOCK
    rows = 2 * BLOCK
    qi = lax.broadcasted_iota(jnp.int32, (rows, 3 * BLOCK), 0) % BLOCK
    ci = lax.broadcasted_iota(jnp.int32, (rows, 3 * BLOCK), 1)
    band = (ci >= qi) & (ci <= qi + 2 * BLOCK)
    row = lax.broadcasted_iota(jnp.int32, (rows, 1), 0)
    lane = lax.broadcasted_iota(jnp.int32, (rows, LANES), 1)
    low = lane < A_HEAD_DIM
    neg = jnp.float32(-1e30)
    variants = [(sub, g, e) for sub in range(nsub) for g in range(2) for e in range(2)]

    def window(prev_ref, cur_ref, next_ref, sub, sl):
        blocks = [prev_ref[:, sl]] + [cur_ref[i * BLOCK:(i + 1) * BLOCK, sl] for i in range(nsub)] \
            + [next_ref[:, sl]]
        return jnp.concatenate(blocks[sub:sub + 3], axis=0)

    scores = []
    for sub, g, e in variants:
        qs = slice(sub * BLOCK, (sub + 1) * BLOCK)
        q = jnp.concatenate([q_ref[qs, (2 * g) * LANES:(2 * g + 1) * LANES],
                             q_ref[qs, (2 * g + 1) * LANES:(2 * g + 2) * LANES]], axis=0)
        sl = slice((2 * g + e) * LANES, (2 * g + e + 1) * LANES)
        scores.append(_dot_nt(q, window(kp_ref, kc_ref, kn_ref, sub, sl)))
    outs = {}
    for (sub, g, e), s in zip(variants, scores):
        blk = nsub * n + sub
        valid = band & ((ci >= BLOCK) | (blk > 0)) & ((ci < 2 * BLOCK) | (blk < nb - 1))
        sl = slice((2 * g + e) * LANES, (2 * g + e + 1) * LANES)
        sink = jnp.where(row < BLOCK, sink_ref[layer, 4 * g + e], sink_ref[layer, 4 * g + 2 + e]) * LOG2E
        s = jnp.where(valid, s, neg)
        s_fold = jnp.maximum(jnp.maximum(s[:, :BLOCK], s[:, BLOCK:2 * BLOCK]), s[:, 2 * BLOCK:])
        m = jnp.maximum(jnp.max(s_fold, axis=-1, keepdims=True), sink)
        p = jnp.exp2(s - m)
        p_fold = p[:, :BLOCK] + p[:, BLOCK:2 * BLOCK] + p[:, 2 * BLOCK:]
        l = jnp.sum(p_fold, axis=-1, keepdims=True) + jnp.exp2(sink - m)
        outs[sub, g, e] = _dot(_bf(p), window(vp_ref, vc_ref, vn_ref, sub, sl)) * (1.0 / l)
    for sub in range(nsub):
        qs = slice(sub * BLOCK, (sub + 1) * BLOCK)
        for g in range(2):
            o = jnp.where(low, outs[sub, g, 0], outs[sub, g, 1])
            for jj in range(2):
                j = 2 * g + jj
                o_ref[qs, j * LANES:(j + 1) * LANES] = _bf(
                    o[jj * BLOCK:(jj + 1) * BLOCK, :] * gate_ref[qs, j * LANES:(j + 1) * LANES])


def _attn_a_call(qa, ka4, va4, gates, sink, layer, B, S, nsub):
    T = qa.shape[0]
    nb = S // BLOCK
    nstep = nb // nsub
    tq = nsub * BLOCK
    cur = lambda b, n: (b * nstep + n, 0)
    prv = lambda b, n: (b * nb + jnp.maximum(nsub * n - 1, 0), 0)
    nxt = lambda b, n: (b * nb + jnp.minimum(nsub * n + nsub, nb - 1), 0)
    w = 4 * LANES
    return pl.pallas_call(
        functools.partial(_attn_a_kernel, nb=nb, layer=layer),
        out_shape=jax.ShapeDtypeStruct((T, A_WIDTH), jnp.bfloat16),
        grid=(B, nstep),
        in_specs=[
            pl.BlockSpec(memory_space=pltpu.SMEM),
            pl.BlockSpec((tq, A_WIDTH), cur),
            pl.BlockSpec((BLOCK, w), prv), pl.BlockSpec((tq, w), cur), pl.BlockSpec((BLOCK, w), nxt),
            pl.BlockSpec((BLOCK, w), prv), pl.BlockSpec((tq, w), cur), pl.BlockSpec((BLOCK, w), nxt),
            pl.BlockSpec((tq, A_WIDTH), cur),
        ],
        out_specs=pl.BlockSpec((tq, A_WIDTH), cur),
        compiler_params=pltpu.CompilerParams(
            dimension_semantics=("arbitrary", "arbitrary"), vmem_limit_bytes=VMEM_LIMIT),
        name="attn_a",
    )(sink, qa, ka4, ka4, ka4, va4, va4, va4, gates)


def _attn_b_kernel(q_ref, k_ref, vt_ref, gate_ref, o_ref, *, tk, lag):
    S = k_ref.shape[0]
    tq = q_ref.shape[0]
    nk = S // tk
    nh = q_ref.shape[1] // LANES
    outs = []
    for hh in range(nh):
        qt = _bf(q_ref[:, hh * LANES:(hh + 1) * LANES].astype(jnp.float32).T)
        m_run = jnp.full((1, tq), -1e30, jnp.float32)
        acc = jnp.zeros((VT_ROWS, tq), jnp.float32)
        inflight = []
        for c in range(nk + lag):
            if c < nk:
                ks = k_ref[c * tk:(c + 1) * tk, hh * LANES:(hh + 1) * LANES]
                inflight.append(_dot(ks, qt))
            if c >= lag:
                cc = c - lag
                st = inflight.pop(0)
                m_new = jnp.maximum(m_run, jnp.max(st, axis=0, keepdims=True))
                alpha = jnp.exp2(m_run - m_new)
                p = _bf(jnp.exp2(st - m_new))
                vt = vt_ref[0, hh * VT_ROWS:(hh + 1) * VT_ROWS, cc * tk:(cc + 1) * tk]
                acc = acc * alpha + _dot(vt, p)
                m_run = m_new
        outs.append(acc[:B_V_DIM, :] * (1.0 / acc[B_V_DIM:B_V_DIM + 1, :]))
    o = jnp.concatenate(outs, axis=0).T
    o_ref[...] = _bf(o * gate_ref[...])


def _attn_b_call(qb, kb, vt, gates, B, S, tq, tk, lag, nh):
    T = qb.shape[0]
    nq = S // tq
    ngrp = B_HEADS // nh
    ow = nh * B_V_DIM
    gate_col0 = A_WIDTH // ow
    return pl.pallas_call(
        functools.partial(_attn_b_kernel, tk=tk, lag=lag),
        out_shape=jax.ShapeDtypeStruct((T, B_WIDTH), jnp.bfloat16),
        grid=(B, ngrp, nq),
        in_specs=[
            pl.BlockSpec((tq, nh * LANES), lambda b, j, i: (b * nq + i, j)),
            pl.BlockSpec((S, nh * LANES), lambda b, j, i: (b, j)),
            pl.BlockSpec((1, nh * VT_ROWS, S), lambda b, j, i: (b, j, 0)),
            pl.BlockSpec((tq, ow), lambda b, j, i: (b * nq + i, gate_col0 + j)),
        ],
        out_specs=pl.BlockSpec((tq, ow), lambda b, j, i: (b * nq + i, j)),
        compiler_params=pltpu.CompilerParams(
            dimension_semantics=("arbitrary", "arbitrary", "arbitrary"), vmem_limit_bytes=VMEM_LIMIT),
        name="attn_b",
    )(qb, kb, vt, gates)


def _out_kernel(x_ref, oa_ref, ob_ref, wout_ref, pg_ref, wgate_ref, p_ref, wproj_ref, y_ref):
    mix = jnp.concatenate([oa_ref[...], ob_ref[...]], axis=-1)
    x1 = x_ref[...] + _dot(mix, wout_ref[...])
    hn = _rms(x1, D_MODEL) * pg_ref[...]
    gate = jax.nn.sigmoid(_dot(_bf(hn), wgate_ref[...]))
    y_ref[...] = x1 + gate * _dot(_bf(p_ref[...]), wproj_ref[...])


def _out_call(x2d, oa, ob, layer, wout, pg, wgate, p3d, wproj, tm):
    T = x2d.shape[0]
    per_layer = lambda w: _layer_spec(w, layer)
    rows = lambda w: pl.BlockSpec((tm, w), lambda i: (i, 0))
    return pl.pallas_call(
        _out_kernel,
        out_shape=jax.ShapeDtypeStruct((T, D_MODEL), jnp.float32),
        grid=(T // tm,),
        in_specs=[rows(D_MODEL), rows(A_WIDTH), rows(B_WIDTH), per_layer(wout),
                  per_layer(pg), per_layer(wgate),
                  pl.BlockSpec((None, tm, PLE_DIM), lambda i: (layer, i, 0)), per_layer(wproj)],
        out_specs=rows(D_MODEL),
        compiler_params=pltpu.CompilerParams(
            dimension_semantics=("arbitrary",), vmem_limit_bytes=VMEM_LIMIT),
        name="out_proj",
    )(x2d, oa, ob, wout, pg, wgate, p3d, wproj)


def _prep_weights(w_in, b_w_uq, b_w_ukv, a_q_norm, a_k_norm, b_q_norm, b_k_norm):
    depth = w_in.shape[0]
    aq, ak, av, ag, bcq, bckv, bkr, bg = jnp.split(w_in, [512, 640, 768, 1280, 1664, 1920, 1952], axis=-1)
    aq_src = np.concatenate([j * LANES + A_SRC for j in range(A_WIDTH // LANES)])
    kr_src = np.where(B_ROPE_LANE, B_SRC - B_NOPE, -1)
    wa = _bf(jnp.concatenate([_take(aq, aq_src), _take(ak, A_SRC), av], axis=-1))
    wc = _bf(jnp.concatenate([bcq, bckv, _take(bkr, kr_src)], axis=-1))
    wg = _bf(jnp.concatenate([ag, bg], axis=-1))
    uq = _take(b_w_uq.reshape(depth, B_Q_LORA, B_HEADS, B_QK_DIM), B_SRC)
    wuq = _bf(uq.reshape(depth, B_Q_LORA, B_HEADS * LANES))
    uq_rot = jnp.where(jnp.asarray(B_ROPE_LANE), jnp.roll(uq, HALF, axis=-1), 0.0)
    wuqr = _bf(uq_rot.reshape(depth, B_Q_LORA, B_HEADS * LANES))
    ukv = b_w_ukv.reshape(depth, B_KV_LORA, B_HEADS, B_NOPE + B_V_DIM)
    kn_src = np.where(B_ROPE_LANE, -1, B_SRC)
    kn = _take(ukv[..., :B_NOPE], kn_src).reshape(depth, B_KV_LORA, B_HEADS * LANES)
    wukv = _bf(jnp.concatenate([kn, ukv[..., B_NOPE:].reshape(depth, B_KV_LORA, B_WIDTH)], axis=-1))

    qscale = (A_HEAD_DIM ** -0.5) * LOG2E
    bscale = (B_QK_DIM ** -0.5) * LOG2E
    gaq = _take(a_q_norm, A_GAIN) * qscale
    gak = _take(a_k_norm, A_GAIN)
    gbq = _take(b_q_norm, B_SRC) * bscale
    gbk = _take(b_k_norm, B_SRC)
    rolled = lambda g: jnp.roll(g, HALF, axis=-1)
    rolled_b = lambda g: jnp.where(jnp.asarray(B_ROPE_LANE), rolled(g), 0.0)
    rows = jnp.stack([gaq, rolled(gaq), gak, rolled(gak), gbq, rolled_b(gbq), gbk, rolled_b(gbk)], axis=1)
    return wc, wa, wg, wuq, wuqr, wukv, rows


def kernel(x, p, positions, norm_g, w_in, a_q_norm, a_k_norm, a_sink, b_cq_norm, b_ckv_norm,
           b_w_uq, b_w_ukv, b_q_norm, b_k_norm, w_out, ple_g, ple_w_gate, ple_w_proj):
    B, S, _ = x.shape
    depth = w_in.shape[0]
    T = B * S

    tabs = _rope_tables(positions)
    wc, wa, wg, wuq, wuqr, wukv, rows = _prep_weights(w_in, b_w_uq, b_w_ukv, a_q_norm, a_k_norm, b_q_norm, b_k_norm)
    lane = np.arange(2 * LANES)
    same_head = ((lane[:, None] // LANES) == (lane[None, :] // LANES)) & \
                (((lane[:, None] // 32) % 2) == ((lane[None, :] // 32) % 2))
    g256 = jnp.asarray(same_head, jnp.bfloat16)
    g128 = g256[:LANES, :LANES]
    wout_b, wgate_b, wproj_b = _bf(w_out), _bf(ple_w_gate), _bf(ple_w_proj)

    ng, cqg, ckvg, pg = (v[:, None, :] for v in (norm_g, b_cq_norm, b_ckv_norm, ple_g))
    sink = a_sink.astype(jnp.float32)
    xc = x.reshape(T, D_MODEL)
    p3 = p.reshape(depth, T, PLE_DIM)
    for i in range(depth):
        qa, ka4, va4, gates, qb, kb, vt = _proj_call(
            xc, B, S, i, ng, wc, wa, wg, tabs, rows, cqg, ckvg, wuq, wuqr, wukv, g256, g128, tm=PROJ_ROWS)
        oa = _attn_a_call(qa, ka4, va4, gates, sink, i, B, S, nsub=A_BLOCKS_PER_STEP)
        ob = _attn_b_call(qb, kb, vt, gates, B, S, tq=B_QUERIES_PER_STEP, tk=B_KEY_CHUNK,
                          lag=B_SCORE_LEAD, nh=B_HEADS_PER_STEP)
        xc = _out_call(xc, oa, ob, i, wout_b, pg, wgate_b, p3, wproj_b, tm=OUT_ROWS)
    return xc.reshape(B, S, D_MODEL)
```

```python
import functools
import math

import jax
import jax.numpy as jnp
import numpy as np
from jax import lax
from jax.experimental import pallas as pl
from jax.experimental.pallas import tpu as pltpu

D_MODEL = 1024
PLE_DIM = 256
ROPE_THETA = 10000.0
NORM_EPS = 1e-6
BLOCK = 128
A_HEADS = 8
A_HEAD_DIM = 64
A_WIDTH = 512
B_HEADS = 8
B_Q_LORA = 384
B_KV_LORA = 256
B_NOPE = 64
B_ROPE = 32
B_QK_DIM = 96
B_V_DIM = 64
B_WIDTH = 512
LANES = 128
HALF = LANES // 2
LOG2E = math.log2(math.e)
VT_ROWS = 80
VMEM_LIMIT = 56 * 1024 * 1024

PROJ_ROWS = 512
OUT_ROWS = 1024
A_BLOCKS_PER_STEP = 4
B_QUERIES_PER_STEP = 1024
B_HEADS_PER_STEP = 2
B_KEY_CHUNK = 128
B_SCORE_LEAD = 8

_L = np.arange(LANES)
A_SRC = ((_L // 32) % 2) * A_HEAD_DIM + (_L // HALF) * 32 + (_L % 32)
A_GAIN = (_L // HALF) * 32 + (_L % 32)
A_FREQ = _L % 32
A_SIGN = np.where(_L < HALF, -1.0, 1.0).astype(np.float32)
B_SRC = np.full(LANES, -1)
B_SRC[0:32:2] = B_NOPE + np.arange(16)
B_SRC[HALF:HALF + 32:2] = B_NOPE + 16 + np.arange(16)
_free = [l for l in range(B_QK_DIM) if B_SRC[l] < 0]
B_SRC[_free] = np.arange(B_NOPE)
B_ROPE_LANE = (B_SRC >= B_NOPE)


def _bf(x):
    return x.astype(jnp.bfloat16)


def _dot(a, b):
    return jnp.dot(a, b, preferred_element_type=jnp.float32)


def _dot_nt(a, b):
    return lax.dot_general(a, b, (((1,), (1,)), ((), ())), preferred_element_type=jnp.float32)


def _take(w, src):
    cols = jnp.take(w, jnp.asarray(np.maximum(src, 0)), axis=-1)
    return jnp.where(jnp.asarray(src >= 0), cols, 0.0)


def _table_kernel(pos_ref, inv_ref, sign_ref, cos_ref, sin_ref):
    ang = pos_ref[...].astype(jnp.float32) * inv_ref[...]
    cos_ref[...] = jnp.cos(ang)
    sin_ref[...] = jnp.sin(ang) * sign_ref[...]


def _rope_tables(positions):
    T = positions.size
    pos = positions.reshape(T, 1)
    inv_a = ROPE_THETA ** (-jnp.arange(A_HEAD_DIM // 2, dtype=jnp.float32) * 2.0 / A_HEAD_DIM)
    inv_row = inv_a[jnp.asarray(A_FREQ)][None, :]
    sign_row = jnp.asarray(A_SIGN)[None, :]
    tt = 1024
    row = pl.BlockSpec((1, LANES), lambda i: (0, 0))
    tab = pl.BlockSpec((tt, LANES), lambda i: (i, 0))
    shp = jax.ShapeDtypeStruct((T, LANES), jnp.float32)
    return pl.pallas_call(
        _table_kernel,
        out_shape=(shp, shp),
        grid=(T // tt,),
        in_specs=[pl.BlockSpec((tt, 1), lambda i: (i, 0)), row, row],
        out_specs=(tab, tab),
        name="rope_tables",
    )(pos, inv_row, sign_row)


def _group_sumsq(x, g_ref):
    x2 = x * x
    hi = _bf(x2)
    lo = _bf(x2 - hi.astype(jnp.float32))
    g = g_ref[...]
    return _dot(hi, g) + _dot(lo, g)


def _swap_halves(x):
    return pltpu.roll(x, HALF, 1)


def _sumsq(x):
    return jnp.sum(x * x, axis=-1, keepdims=True)


def _rms(x, n):
    return x * lax.rsqrt(_sumsq(x) * (1.0 / n) + NORM_EPS)


R_AQ, R_AQ_ROT, R_AK, R_AK_ROT, R_BQ, R_BQ_ROT, R_BK, R_BK_ROT = range(8)


def _proj_kernel(x_ref, ng_ref, wc_ref, wa_ref, wg_ref, cosa_ref, sina_ref, rows_ref,
                 cqg_ref, ckvg_ref, wuq_ref, wuqr_ref, wukv_ref, g256_ref, g128_ref,
                 qa_ref, ka4_ref, va4_ref, gates_ref, qb_ref, kb_ref, vt_ref):
    hb = _bf(_rms(x_ref[...], D_MODEL) * ng_ref[...])
    zc = _dot(hb, wc_ref[...])
    za = _dot(hb, wa_ref[...])
    cqn = _bf(_rms(zc[:, :B_Q_LORA], B_Q_LORA) * cqg_ref[...])
    ckvn = _bf(_rms(zc[:, B_Q_LORA:B_Q_LORA + B_KV_LORA], B_KV_LORA) * ckvg_ref[...])
    kr = zc[:, B_Q_LORA + B_KV_LORA:]
    qf = _dot(cqn, wuq_ref[...])
    qrot = _dot(cqn, wuqr_ref[...])
    kv = _dot(ckvn, wukv_ref[...])
    aq = za[:, :A_WIDTH]
    ak = za[:, A_WIDTH:A_WIDTH + LANES]
    w2 = 2 * LANES
    ssq_a = jnp.concatenate([_group_sumsq(aq[:, :w2], g256_ref), _group_sumsq(aq[:, w2:], g256_ref)], axis=-1)
    ssk_a = _group_sumsq(ak, g128_ref)
    zg = _dot(hb, wg_ref[...])

    def row(r):
        return rows_ref[r:r + 1, :]

    cosa, sina = cosa_ref[...], sina_ref[...]
    lane_t = lax.broadcasted_iota(jnp.int32, cosa.shape, 1)
    cosb = jnp.where((lane_t % 2 == 0) & (lane_t % HALF < B_ROPE), cosa, 1.0)
    sinb = sina
    inv_qk = 1.0 / B_QK_DIM

    cq, sq = cosb * row(R_BQ), sinb * row(R_BQ_ROT)
    for hd in range(B_HEADS):
        sl = slice(hd * LANES, (hd + 1) * LANES)
        xq = qf[:, sl]
        r = lax.rsqrt(_sumsq(xq) * inv_qk + NORM_EPS)
        qb_ref[:, sl] = _bf((xq * cq + qrot[:, sl] * sq) * r)

    krr = kr * (cosb * row(R_BK)) + _swap_halves(kr) * (sinb * row(R_BK_ROT))
    ss_kr = _sumsq(kr)
    gk = row(R_BK)
    for hd in range(B_HEADS):
        sl = slice(hd * LANES, (hd + 1) * LANES)
        kn = kv[:, sl]
        r = lax.rsqrt((_sumsq(kn) + ss_kr) * inv_qk + NORM_EPS)
        kb_ref[:, sl] = _bf((kn * gk + krr) * r)

    vt = kv[:, B_HEADS * LANES:].T
    ones = jnp.ones((VT_ROWS - B_V_DIM, vt.shape[1]), jnp.bfloat16)
    for hd in range(B_HEADS):
        vt_ref[0, hd * VT_ROWS:hd * VT_ROWS + B_V_DIM, :] = _bf(vt[hd * B_V_DIM:(hd + 1) * B_V_DIM, :])
        vt_ref[0, hd * VT_ROWS + B_V_DIM:(hd + 1) * VT_ROWS, :] = ones

    inv_hd = 1.0 / A_HEAD_DIM
    rq = lax.rsqrt(ssq_a * inv_hd + NORM_EPS)
    ca, sa = cosa * row(R_AQ), sina * row(R_AQ_ROT)
    for c in range(A_WIDTH // LANES):
        sl = slice(c * LANES, (c + 1) * LANES)
        xs = aq[:, sl]
        qa_ref[:, sl] = _bf((xs * ca + _swap_halves(xs) * sa) * rq[:, sl])

    rk = lax.rsqrt(ssk_a * inv_hd + NORM_EPS)
    akr = (ak * (cosa * row(R_AK)) + _swap_halves(ak) * (sina * row(R_AK_ROT))) * rk
    lane = lax.broadcasted_iota(jnp.int32, ak.shape, 1)
    even = ((lane // 32) % 2) == 0
    ke = jnp.where(even, akr, 0.0)
    ko = jnp.where(even, 0.0, akr)
    ka4_ref[:, 0 * LANES:1 * LANES] = _bf(ke)
    ka4_ref[:, 1 * LANES:2 * LANES] = _bf(pltpu.roll(ke, 32, 1))
    ka4_ref[:, 2 * LANES:3 * LANES] = _bf(pltpu.roll(ko, LANES - 32, 1))
    ka4_ref[:, 3 * LANES:4 * LANES] = _bf(ko)
    av = za[:, A_WIDTH + LANES:]
    low = lane < A_HEAD_DIM
    vlo = jnp.where(low, av, 0.0)
    vhi = jnp.where(low, 0.0, av)
    va4_ref[:, 0 * LANES:1 * LANES] = _bf(vlo)
    va4_ref[:, 1 * LANES:2 * LANES] = _bf(_swap_halves(vlo))
    va4_ref[:, 2 * LANES:3 * LANES] = _bf(_swap_halves(vhi))
    va4_ref[:, 3 * LANES:4 * LANES] = _bf(vhi)

    gates_ref[...] = zg * jax.nn.sigmoid(zg)


def _layer_spec(stacked, layer):
    tail = stacked.shape[1:]
    return pl.BlockSpec((None,) + tail, lambda *_: (layer,) + (0,) * len(tail))


def _proj_call(x2d, B, S, layer, ng, wc, wa, wg, tabs, rows, cqg, ckvg, wuq, wuqr, wukv, g256, g128, tm):
    T = x2d.shape[0]
    nt = S // tm
    cosa, sina = tabs
    const = lambda shape: pl.BlockSpec(shape, lambda i: (0,) * len(shape))
    per_layer = lambda w: _layer_spec(w, layer)
    rowblk = lambda w: pl.BlockSpec((tm, w), lambda i: (i, 0))
    bf = jnp.bfloat16
    out_shape = (
        jax.ShapeDtypeStruct((T, A_WIDTH), bf),
        jax.ShapeDtypeStruct((T, 4 * LANES), bf),
        jax.ShapeDtypeStruct((T, 4 * LANES), bf),
        jax.ShapeDtypeStruct((T, A_WIDTH + B_WIDTH), jnp.float32),
        jax.ShapeDtypeStruct((T, B_HEADS * LANES), bf),
        jax.ShapeDtypeStruct((T, B_HEADS * LANES), bf),
        jax.ShapeDtypeStruct((B, B_HEADS * VT_ROWS, S), bf),
    )
    out_specs = (
        rowblk(A_WIDTH), rowblk(4 * LANES), rowblk(4 * LANES), rowblk(A_WIDTH + B_WIDTH),
        rowblk(B_HEADS * LANES), rowblk(B_HEADS * LANES),
        pl.BlockSpec((1, B_HEADS * VT_ROWS, tm), lambda i: (i // nt, 0, i % nt)),
    )
    in_specs = [
        rowblk(D_MODEL), per_layer(ng),
        per_layer(wc), per_layer(wa), per_layer(wg),
        rowblk(LANES), rowblk(LANES),
        per_layer(rows), per_layer(cqg), per_layer(ckvg),
        per_layer(wuq), per_layer(wuqr), per_layer(wukv), const(g256.shape), const(g128.shape),
    ]
    return pl.pallas_call(
        _proj_kernel,
        out_shape=out_shape,
        grid=(T // tm,),
        in_specs=in_specs,
        out_specs=out_specs,
        compiler_params=pltpu.CompilerParams(
            dimension_semantics=("arbitrary",), vmem_limit_bytes=VMEM_LIMIT),
        name="proj",
    )(x2d, ng, wc, wa, wg, cosa, sina, rows, cqg, ckvg, wuq, wuqr, wukv, g256, g128)


def _attn_a_kernel(sink_ref, q_ref, kp_ref, kc_ref, kn_ref, vp_ref, vc_ref, vn_ref, gate_ref, o_ref, *,
                   nb, layer):
    n = pl.program_id(1)
    nsub = q_ref.shape[0] // BLOCK
    rows = 2 * BLOCK
    qi = lax.broadcasted_iota(jnp.int32, (rows, 3 * BLOCK), 0) % BLOCK
    ci = lax.broadcasted_iota(jnp.int32, (rows, 3 * BLOCK), 1)
    band = (ci >= qi) & (ci <= qi + 2 * BLOCK)
    row = lax.broadcasted_iota(jnp.int32, (rows, 1), 0)
    lane = lax.broadcasted_iota(jnp.int32, (rows, LANES), 1)
    low = lane < A_HEAD_DIM
    neg = jnp.float32(-1e30)
    variants = [(sub, g, e) for sub in range(nsub) for g in range(2) for e in range(2)]

    def window(prev_ref, cur_ref, next_ref, sub, sl):
        blocks = [prev_ref[:, sl]] + [cur_ref[i * BLOCK:(i + 1) * BLOCK, sl] for i in range(nsub)] \
            + [next_ref[:, sl]]
        return jnp.concatenate(blocks[sub:sub + 3], axis=0)

    scores = []
    for sub, g, e in variants:
        qs = slice(sub * BLOCK, (sub + 1) * BLOCK)
        q = jnp.concatenate([q_ref[qs, (2 * g) * LANES:(2 * g + 1) * LANES],
                             q_ref[qs, (2 * g + 1) * LANES:(2 * g + 2) * LANES]], axis=0)
        sl = slice((2 * g + e) * LANES, (2 * g + e + 1) * LANES)
        scores.append(_dot_nt(q, window(kp_ref, kc_ref, kn_ref, sub, sl)))
    outs = {}
    for (sub, g, e), s in zip(variants, scores):
        blk = nsub * n + sub
        valid = band & ((ci >= BLOCK) | (blk > 0)) & ((ci < 2 * BLOCK) | (blk < nb - 1))
        sl = slice((2 * g + e) * LANES, (2 * g + e + 1) * LANES)
        sink = jnp.where(row < BLOCK, sink_ref[layer, 4 * g + e], sink_ref[layer, 4 * g + 2 + e]) * LOG2E
        s = jnp.where(valid, s, neg)
        s_fold = jnp.maximum(jnp.maximum(s[:, :BLOCK], s[:, BLOCK:2 * BLOCK]), s[:, 2 * BLOCK:])
        m = jnp.maximum(jnp.max(s_fold, axis=-1, keepdims=True), sink)
        p = jnp.exp2(s - m)
        p_fold = p[:, :BLOCK] + p[:, BLOCK:2 * BLOCK] + p[:, 2 * BLOCK:]
        l = jnp.sum(p_fold, axis=-1, keepdims=True) + jnp.exp2(sink - m)
        outs[sub, g, e] = _dot(_bf(p), window(vp_ref, vc_ref, vn_ref, sub, sl)) * (1.0 / l)
    for sub in range(nsub):
        qs = slice(sub * BLOCK, (sub + 1) * BLOCK)
        for g in range(2):
            o = jnp.where(low, outs[sub, g, 0], outs[sub, g, 1])
            for jj in range(2):
                j = 2 * g + jj
                o_ref[qs, j * LANES:(j + 1) * LANES] = _bf(
                    o[jj * BLOCK:(jj + 1) * BLOCK, :] * gate_ref[qs, j * LANES:(j + 1) * LANES])


def _attn_a_call(qa, ka4, va4, gates, sink, layer, B, S, nsub):
    T = qa.shape[0]
    nb = S // BLOCK
    nstep = nb // nsub
    tq = nsub * BLOCK
    cur = lambda b, n: (b * nstep + n, 0)
    prv = lambda b, n: (b * nb + jnp.maximum(nsub * n - 1, 0), 0)
    nxt = lambda b, n: (b * nb + jnp.minimum(nsub * n + nsub, nb - 1), 0)
    w = 4 * LANES
    return pl.pallas_call(
        functools.partial(_attn_a_kernel, nb=nb, layer=layer),
        out_shape=jax.ShapeDtypeStruct((T, A_WIDTH), jnp.bfloat16),
        grid=(B, nstep),
        in_specs=[
            pl.BlockSpec(memory_space=pltpu.SMEM),
            pl.BlockSpec((tq, A_WIDTH), cur),
            pl.BlockSpec((BLOCK, w), prv), pl.BlockSpec((tq, w), cur), pl.BlockSpec((BLOCK, w), nxt),
            pl.BlockSpec((BLOCK, w), prv), pl.BlockSpec((tq, w), cur), pl.BlockSpec((BLOCK, w), nxt),
            pl.BlockSpec((tq, A_WIDTH), cur),
        ],
        out_specs=pl.BlockSpec((tq, A_WIDTH), cur),
        compiler_params=pltpu.CompilerParams(
            dimension_semantics=("arbitrary", "arbitrary"), vmem_limit_bytes=VMEM_LIMIT),
        name="attn_a",
    )(sink, qa, ka4, ka4, ka4, va4, va4, va4, gates)


def _attn_b_kernel(q_ref, k_ref, vt_ref, gate_ref, o_ref, *, tk, lag):
    S = k_ref.shape[0]
    tq = q_ref.shape[0]
    nk = S // tk
    nh = q_ref.shape[1] // LANES
    outs = []
    for hh in range(nh):
        qt = _bf(q_ref[:, hh * LANES:(hh + 1) * LANES].astype(jnp.float32).T)
        m_run = jnp.full((1, tq), -1e30, jnp.float32)
        acc = jnp.zeros((VT_ROWS, tq), jnp.float32)
        inflight = []
        for c in range(nk + lag):
            if c < nk:
                ks = k_ref[c * tk:(c + 1) * tk, hh * LANES:(hh + 1) * LANES]
                inflight.append(_dot(ks, qt))
            if c >= lag:
                cc = c - lag
                st = inflight.pop(0)
                m_new = jnp.maximum(m_run, jnp.max(st, axis=0, keepdims=True))
                alpha = jnp.exp2(m_run - m_new)
                p = _bf(jnp.exp2(st - m_new))
                vt = vt_ref[0, hh * VT_ROWS:(hh + 1) * VT_ROWS, cc * tk:(cc + 1) * tk]
                acc = acc * alpha + _dot(vt, p)
                m_run = m_new
        outs.append(acc[:B_V_DIM, :] * (1.0 / acc[B_V_DIM:B_V_DIM + 1, :]))
    o = jnp.concatenate(outs, axis=0).T
    o_ref[...] = _bf(o * gate_ref[...])


def _attn_b_call(qb, kb, vt, gates, B, S, tq, tk, lag, nh):
    T = qb.shape[0]
    nq = S // tq
    ngrp = B_HEADS // nh
    ow = nh * B_V_DIM
    gate_col0 = A_WIDTH // ow
    return pl.pallas_call(
        functools.partial(_attn_b_kernel, tk=tk, lag=lag),
        out_shape=jax.ShapeDtypeStruct((T, B_WIDTH), jnp.bfloat16),
        grid=(B, ngrp, nq),
        in_specs=[
            pl.BlockSpec((tq, nh * LANES), lambda b, j, i: (b * nq + i, j)),
            pl.BlockSpec((S, nh * LANES), lambda b, j, i: (b, j)),
            pl.BlockSpec((1, nh * VT_ROWS, S), lambda b, j, i: (b, j, 0)),
            pl.BlockSpec((tq, ow), lambda b, j, i: (b * nq + i, gate_col0 + j)),
        ],
        out_specs=pl.BlockSpec((tq, ow), lambda b, j, i: (b * nq + i, j)),
        compiler_params=pltpu.CompilerParams(
            dimension_semantics=("arbitrary", "arbitrary", "arbitrary"), vmem_limit_bytes=VMEM_LIMIT),
        name="attn_b",
    )(qb, kb, vt, gates)


def _out_kernel(x_ref, oa_ref, ob_ref, wout_ref, pg_ref, wgate_ref, p_ref, wproj_ref, y_ref):
    mix = jnp.concatenate([oa_ref[...], ob_ref[...]], axis=-1)
    x1 = x_ref[...] + _dot(mix, wout_ref[...])
    hn = _rms(x1, D_MODEL) * pg_ref[...]
    gate = jax.nn.sigmoid(_dot(_bf(hn), wgate_ref[...]))
    y_ref[...] = x1 + gate * _dot(_bf(p_ref[...]), wproj_ref[...])


def _out_call(x2d, oa, ob, layer, wout, pg, wgate, p3d, wproj, tm):
    T = x2d.shape[0]
    per_layer = lambda w: _layer_spec(w, layer)
    rows = lambda w: pl.BlockSpec((tm, w), lambda i: (i, 0))
    return pl.pallas_call(
        _out_kernel,
        out_shape=jax.ShapeDtypeStruct((T, D_MODEL), jnp.float32),
        grid=(T // tm,),
        in_specs=[rows(D_MODEL), rows(A_WIDTH), rows(B_WIDTH), per_layer(wout),
                  per_layer(pg), per_layer(wgate),
                  pl.BlockSpec((None, tm, PLE_DIM), lambda i: (layer, i, 0)), per_layer(wproj)],
        out_specs=rows(D_MODEL),
        compiler_params=pltpu.CompilerParams(
            dimension_semantics=("arbitrary",), vmem_limit_bytes=VMEM_LIMIT),
        name="out_proj",
    )(x2d, oa, ob, wout, pg, wgate, p3d, wproj)


def _prep_weights(w_in, b_w_uq, b_w_ukv, a_q_norm, a_k_norm, b_q_norm, b_k_norm):
    depth = w_in.shape[0]
    aq, ak, av, ag, bcq, bckv, bkr, bg = jnp.split(w_in, [512, 640, 768, 1280, 1664, 1920, 1952], axis=-1)
    aq_src = np.concatenate([j * LANES + A_SRC for j in range(A_WIDTH // LANES)])
    kr_src = np.where(B_ROPE_LANE, B_SRC - B_NOPE, -1)
    wa = _bf(jnp.concatenate([_take(aq, aq_src), _take(ak, A_SRC), av], axis=-1))
    wc = _bf(jnp.concatenate([bcq, bckv, _take(bkr, kr_src)], axis=-1))
    wg = _bf(jnp.concatenate([ag, bg], axis=-1))
    uq = _take(b_w_uq.reshape(depth, B_Q_LORA, B_HEADS, B_QK_DIM), B_SRC)
    wuq = _bf(uq.reshape(depth, B_Q_LORA, B_HEADS * LANES))
    uq_rot = jnp.where(jnp.asarray(B_ROPE_LANE), jnp.roll(uq, HALF, axis=-1), 0.0)
    wuqr = _bf(uq_rot.reshape(depth, B_Q_LORA, B_HEADS * LANES))
    ukv = b_w_ukv.reshape(depth, B_KV_LORA, B_HEADS, B_NOPE + B_V_DIM)
    kn_src = np.where(B_ROPE_LANE, -1, B_SRC)
    kn = _take(ukv[..., :B_NOPE], kn_src).reshape(depth, B_KV_LORA, B_HEADS * LANES)
    wukv = _bf(jnp.concatenate([kn, ukv[..., B_NOPE:].reshape(depth, B_KV_LORA, B_WIDTH)], axis=-1))

    qscale = (A_HEAD_DIM ** -0.5) * LOG2E
    bscale = (B_QK_DIM ** -0.5) * LOG2E
    gaq = _take(a_q_norm, A_GAIN) * qscale
    gak = _take(a_k_norm, A_GAIN)
    gbq = _take(b_q_norm, B_SRC) * bscale
    gbk = _take(b_k_norm, B_SRC)
    rolled = lambda g: jnp.roll(g, HALF, axis=-1)
    rolled_b = lambda g: jnp.where(jnp.asarray(B_ROPE_LANE), rolled(g), 0.0)
    rows = jnp.stack([gaq, rolled(gaq), gak, rolled(gak), gbq, rolled_b(gbq), gbk, rolled_b(gbk)], axis=1)
    return wc, wa, wg, wuq, wuqr, wukv, rows


def kernel(x, p, positions, norm_g, w_in, a_q_norm, a_k_norm, a_sink, b_cq_norm, b_ckv_norm,
           b_w_uq, b_w_ukv, b_q_norm, b_k_norm, w_out, ple_g, ple_w_gate, ple_w_proj):
    B, S, _ = x.shape
    depth = w_in.shape[0]
    T = B * S

    tabs = _rope_tables(positions)
    wc, wa, wg, wuq, wuqr, wukv, rows = _prep_weights(w_in, b_w_uq, b_w_ukv, a_q_norm, a_k_norm, b_q_norm, b_k_norm)
    lane = np.arange(2 * LANES)
    same_head = ((lane[:, None] // LANES) == (lane[None, :] // LANES)) & \
                (((lane[:, None] // 32) % 2) == ((lane[None, :] // 32) % 2))
    g256 = jnp.asarray(same_head, jnp.bfloat16)
    g128 = g256[:LANES, :LANES]
    wout_b, wgate_b, wproj_b = _bf(w_out), _bf(ple_w_gate), _bf(ple_w_proj)

    ng, cqg, ckvg, pg = (v[:, None, :] for v in (norm_g, b_cq_norm, b_ckv_norm, ple_g))
    sink = a_sink.astype(jnp.float32)
    xc = x.reshape(T, D_MODEL)
    p3 = p.reshape(depth, T, PLE_DIM)
    for i in range(depth):
        qa, ka4, va4, gates, qb, kb, vt = _proj_call(
            xc, B, S, i, ng, wc, wa, wg, tabs, rows, cqg, ckvg, wuq, wuqr, wukv, g256, g128, tm=PROJ_ROWS)
        oa = _attn_a_call(qa, ka4, va4, gates, sink, i, B, S, nsub=A_BLOCKS_PER_STEP)
        ob = _attn_b_call(qb, kb, vt, gates, B, S, tq=B_QUERIES_PER_STEP, tk=B_KEY_CHUNK,
                          lag=B_SCORE_LEAD, nh=B_HEADS_PER_STEP)
        xc = _out_call(xc, oa, ob, i, wout_b, pg, wgate_b, p3, wproj_b, tm=OUT_ROWS)
    return xc.reshape(B, S, D_MODEL)
```

```python
import functools
import math

import jax
import jax.numpy as jnp
import numpy as np
from jax import lax
from jax.experimental import pallas as pl
from jax.experimental.pallas import tpu as pltpu

D_MODEL = 1024
PLE_DIM = 256
ROPE_THETA = 10000.0
NORM_EPS = 1e-6
BLOCK = 128
A_HEADS = 8
A_HEAD_DIM = 64
A_WIDTH = 512
B_HEADS = 8
B_Q_LORA = 384
B_KV_LORA = 256
B_NOPE = 64
B_ROPE = 32
B_QK_DIM = 96
B_V_DIM = 64
B_WIDTH = 512
LANES = 128
HALF = LANES // 2
LOG2E = math.log2(math.e)
VT_ROWS = 80
VMEM_LIMIT = 56 * 1024 * 1024

PROJ_ROWS = 512
OUT_ROWS = 1024
A_BLOCKS_PER_STEP = 4
B_QUERIES_PER_STEP = 1024
B_HEADS_PER_STEP = 4
B_KEY_CHUNK = 128
B_SCORE_LEAD = 8

_L = np.arange(LANES)
A_SRC = ((_L // 32) % 2) * A_HEAD_DIM + (_L // HALF) * 32 + (_L % 32)
A_GAIN = (_L // HALF) * 32 + (_L % 32)
A_FREQ = _L % 32
A_SIGN = np.where(_L < HALF, -1.0, 1.0).astype(np.float32)
B_SRC = np.full(LANES, -1)
B_SRC[0:32:2] = B_NOPE + np.arange(16)
B_SRC[HALF:HALF + 32:2] = B_NOPE + 16 + np.arange(16)
_free = [l for l in range(B_QK_DIM) if B_SRC[l] < 0]
B_SRC[_free] = np.arange(B_NOPE)
B_ROPE_LANE = (B_SRC >= B_NOPE)


def _bf(x):
    return x.astype(jnp.bfloat16)


def _dot(a, b):
    return jnp.dot(a, b, preferred_element_type=jnp.float32)


def _dot_nt(a, b):
    return lax.dot_general(a, b, (((1,), (1,)), ((), ())), preferred_element_type=jnp.float32)


def _take(w, src):
    cols = jnp.take(w, jnp.asarray(np.maximum(src, 0)), axis=-1)
    return jnp.where(jnp.asarray(src >= 0), cols, 0.0)


def _table_kernel(pos_ref, inv_ref, sign_ref, cos_ref, sin_ref):
    ang = pos_ref[...].astype(jnp.float32) * inv_ref[...]
    cos_ref[...] = jnp.cos(ang)
    sin_ref[...] = jnp.sin(ang) * sign_ref[...]


def _rope_tables(positions):
    T = positions.size
    pos = positions.reshape(T, 1)
    inv_a = ROPE_THETA ** (-jnp.arange(A_HEAD_DIM // 2, dtype=jnp.float32) * 2.0 / A_HEAD_DIM)
    inv_row = inv_a[jnp.asarray(A_FREQ)][None, :]
    sign_row = jnp.asarray(A_SIGN)[None, :]
    tt = 1024
    row = pl.BlockSpec((1, LANES), lambda i: (0, 0))
    tab = pl.BlockSpec((tt, LANES), lambda i: (i, 0))
    shp = jax.ShapeDtypeStruct((T, LANES), jnp.float32)
    return pl.pallas_call(
        _table_kernel,
        out_shape=(shp, shp),
        grid=(T // tt,),
        in_specs=[pl.BlockSpec((tt, 1), lambda i: (i, 0)), row, row],
        out_specs=(tab, tab),
        name="rope_tables",
    )(pos, inv_row, sign_row)


def _group_sumsq(x, g_ref):
    x2 = x * x
    hi = _bf(x2)
    lo = _bf(x2 - hi.astype(jnp.float32))
    g = g_ref[...]
    return _dot(hi, g) + _dot(lo, g)


def _swap_halves(x):
    return pltpu.roll(x, HALF, 1)


def _sumsq(x):
    return jnp.sum(x * x, axis=-1, keepdims=True)


def _rms(x, n):
    return x * lax.rsqrt(_sumsq(x) * (1.0 / n) + NORM_EPS)


R_AQ, R_AQ_ROT, R_AK, R_AK_ROT, R_BQ, R_BQ_ROT, R_BK, R_BK_ROT = range(8)


def _proj_kernel(x_ref, ng_ref, wc_ref, wa_ref, wg_ref, cosa_ref, sina_ref, rows_ref,
                 cqg_ref, ckvg_ref, wuq_ref, wuqr_ref, wukv_ref, g256_ref, g128_ref,
                 qa_ref, ka4_ref, va4_ref, gates_ref, qb_ref, kb_ref, vt_ref):
    hb = _bf(_rms(x_ref[...], D_MODEL) * ng_ref[...])
    zc = _dot(hb, wc_ref[...])
    za = _dot(hb, wa_ref[...])
    cqn = _bf(_rms(zc[:, :B_Q_LORA], B_Q_LORA) * cqg_ref[...])
    ckvn = _bf(_rms(zc[:, B_Q_LORA:B_Q_LORA + B_KV_LORA], B_KV_LORA) * ckvg_ref[...])
    kr = zc[:, B_Q_LORA + B_KV_LORA:]
    qf = _dot(cqn, wuq_ref[...])
    qrot = _dot(cqn, wuqr_ref[...])
    kv = _dot(ckvn, wukv_ref[...])
    aq = za[:, :A_WIDTH]
    ak = za[:, A_WIDTH:A_WIDTH + LANES]
    w2 = 2 * LANES
    ssq_a = jnp.concatenate([_group_sumsq(aq[:, :w2], g256_ref), _group_sumsq(aq[:, w2:], g256_ref)], axis=-1)
    ssk_a = _group_sumsq(ak, g128_ref)
    zg = _dot(hb, wg_ref[...])

    def row(r):
        return rows_ref[r:r + 1, :]

    cosa, sina = cosa_ref[...], sina_ref[...]
    lane_t = lax.broadcasted_iota(jnp.int32, cosa.shape, 1)
    cosb = jnp.where((lane_t % 2 == 0) & (lane_t % HALF < B_ROPE), cosa, 1.0)
    sinb = sina
    inv_qk = 1.0 / B_QK_DIM

    cq, sq = cosb * row(R_BQ), sinb * row(R_BQ_ROT)
    for hd in range(B_HEADS):
        sl = slice(hd * LANES, (hd + 1) * LANES)
        xq = qf[:, sl]
        r = lax.rsqrt(_sumsq(xq) * inv_qk + NORM_EPS)
        qb_ref[:, sl] = _bf((xq * cq + qrot[:, sl] * sq) * r)

    krr = kr * (cosb * row(R_BK)) + _swap_halves(kr) * (sinb * row(R_BK_ROT))
    ss_kr = _sumsq(kr)
    gk = row(R_BK)
    for hd in range(B_HEADS):
        sl = slice(hd * LANES, (hd + 1) * LANES)
        kn = kv[:, sl]
        r = lax.rsqrt((_sumsq(kn) + ss_kr) * inv_qk + NORM_EPS)
        kb_ref[:, sl] = _bf((kn * gk + krr) * r)

    vt = kv[:, B_HEADS * LANES:].T
    ones = jnp.ones((VT_ROWS - B_V_DIM, vt.shape[1]), jnp.bfloat16)
    for hd in range(B_HEADS):
        vt_ref[0, hd * VT_ROWS:hd * VT_ROWS + B_V_DIM, :] = _bf(vt[hd * B_V_DIM:(hd + 1) * B_V_DIM, :])
        vt_ref[0, hd * VT_ROWS + B_V_DIM:(hd + 1) * VT_ROWS, :] = ones

    inv_hd = 1.0 / A_HEAD_DIM
    rq = lax.rsqrt(ssq_a * inv_hd + NORM_EPS)
    ca, sa = cosa * row(R_AQ), sina * row(R_AQ_ROT)
    for c in range(A_WIDTH // LANES):
        sl = slice(c * LANES, (c + 1) * LANES)
        xs = aq[:, sl]
        qa_ref[:, sl] = _bf((xs * ca + _swap_halves(xs) * sa) * rq[:, sl])

    rk = lax.rsqrt(ssk_a * inv_hd + NORM_EPS)
    akr = (ak * (cosa * row(R_AK)) + _swap_halves(ak) * (sina * row(R_AK_ROT))) * rk
    lane = lax.broadcasted_iota(jnp.int32, ak.shape, 1)
    even = ((lane // 32) % 2) == 0
    ke = jnp.where(even, akr, 0.0)
    ko = jnp.where(even, 0.0, akr)
    ka4_ref[:, 0 * LANES:1 * LANES] = _bf(ke)
    ka4_ref[:, 1 * LANES:2 * LANES] = _bf(pltpu.roll(ke, 32, 1))
    ka4_ref[:, 2 * LANES:3 * LANES] = _bf(pltpu.roll(ko, LANES - 32, 1))
    ka4_ref[:, 3 * LANES:4 * LANES] = _bf(ko)
    av = za[:, A_WIDTH + LANES:]
    low = lane < A_HEAD_DIM
    vlo = jnp.where(low, av, 0.0)
    vhi = jnp.where(low, 0.0, av)
    va4_ref[:, 0 * LANES:1 * LANES] = _bf(vlo)
    va4_ref[:, 1 * LANES:2 * LANES] = _bf(_swap_halves(vlo))
    va4_ref[:, 2 * LANES:3 * LANES] = _bf(_swap_halves(vhi))
    va4_ref[:, 3 * LANES:4 * LANES] = _bf(vhi)

    gates_ref[...] = zg * jax.nn.sigmoid(zg)


def _layer_spec(stacked, layer):
    tail = stacked.shape[1:]
    return pl.BlockSpec((None,) + tail, lambda *_: (layer,) + (0,) * len(tail))


def _proj_call(x2d, B, S, layer, ng, wc, wa, wg, tabs, rows, cqg, ckvg, wuq, wuqr, wukv, g256, g128, tm):
    T = x2d.shape[0]
    nt = S // tm
    cosa, sina = tabs
    const = lambda shape: pl.BlockSpec(shape, lambda i: (0,) * len(shape))
    per_layer = lambda w: _layer_spec(w, layer)
    rowblk = lambda w: pl.BlockSpec((tm, w), lambda i: (i, 0))
    bf = jnp.bfloat16
    out_shape = (
        jax.ShapeDtypeStruct((T, A_WIDTH), bf),
        jax.ShapeDtypeStruct((T, 4 * LANES), bf),
        jax.ShapeDtypeStruct((T, 4 * LANES), bf),
        jax.ShapeDtypeStruct((T, A_WIDTH + B_WIDTH), jnp.float32),
        jax.ShapeDtypeStruct((T, B_HEADS * LANES), bf),
        jax.ShapeDtypeStruct((T, B_HEADS * LANES), bf),
        jax.ShapeDtypeStruct((B, B_HEADS * VT_ROWS, S), bf),
    )
    out_specs = (
        rowblk(A_WIDTH), rowblk(4 * LANES), rowblk(4 * LANES), rowblk(A_WIDTH + B_WIDTH),
        rowblk(B_HEADS * LANES), rowblk(B_HEADS * LANES),
        pl.BlockSpec((1, B_HEADS * VT_ROWS, tm), lambda i: (i // nt, 0, i % nt)),
    )
    in_specs = [
        rowblk(D_MODEL), per_layer(ng),
        per_layer(wc), per_layer(wa), per_layer(wg),
        rowblk(LANES), rowblk(LANES),
        per_layer(rows), per_layer(cqg), per_layer(ckvg),
        per_layer(wuq), per_layer(wuqr), per_layer(wukv), const(g256.shape), const(g128.shape),
    ]
    return pl.pallas_call(
        _proj_kernel,
        out_shape=out_shape,
        grid=(T // tm,),
        in_specs=in_specs,
        out_specs=out_specs,
        compiler_params=pltpu.CompilerParams(
            dimension_semantics=("arbitrary",), vmem_limit_bytes=VMEM_LIMIT),
        name="proj",
    )(x2d, ng, wc, wa, wg, cosa, sina, rows, cqg, ckvg, wuq, wuqr, wukv, g256, g128)


def _attn_a_kernel(sink_ref, q_ref, kp_ref, kc_ref, kn_ref, vp_ref, vc_ref, vn_ref, gate_ref, o_ref, *,
                   nb, layer):
    n = pl.program_id(1)
    nsub = q_ref.shape[0] // BLOCK
    rows = 2 * BLOCK
    qi = lax.broadcasted_iota(jnp.int32, (rows, 3 * BLOCK), 0) % BLOCK
    ci = lax.broadcasted_iota(jnp.int32, (rows, 3 * BLOCK), 1)
    band = (ci >= qi) & (ci <= qi + 2 * BLOCK)
    row = lax.broadcasted_iota(jnp.int32, (rows, 1), 0)
    lane = lax.broadcasted_iota(jnp.int32, (rows, LANES), 1)
    low = lane < A_HEAD_DIM
    neg = jnp.float32(-1e30)
    variants = [(sub, g, e) for sub in range(nsub) for g in range(2) for e in range(2)]

    def window(prev_ref, cur_ref, next_ref, sub, sl):
        blocks = [prev_ref[:, sl]] + [cur_ref[i * BLOCK:(i + 1) * BLOCK, sl] for i in range(nsub)] \
            + [next_ref[:, sl]]
        return jnp.concatenate(blocks[sub:sub + 3], axis=0)

    scores = []
    for sub, g, e in variants:
        qs = slice(sub * BLOCK, (sub + 1) * BLOCK)
        q = jnp.concatenate([q_ref[qs, (2 * g) * LANES:(2 * g + 1) * LANES],
                             q_ref[qs, (2 * g + 1) * LANES:(2 * g + 2) * LANES]], axis=0)
        sl = slice((2 * g + e) * LANES, (2 * g + e + 1) * LANES)
        scores.append(_dot_nt(q, window(kp_ref, kc_ref, kn_ref, sub, sl)))
    outs = {}
    for (sub, g, e), s in zip(variants, scores):
        blk = nsub * n + sub
        valid = band & ((ci >= BLOCK) | (blk > 0)) & ((ci < 2 * BLOCK) | (blk < nb - 1))
        sl = slice((2 * g + e) * LANES, (2 * g + e + 1) * LANES)
        sink = jnp.where(row < BLOCK, sink_ref[layer, 4 * g + e], sink_ref[layer, 4 * g + 2 + e]) * LOG2E
        s = jnp.where(valid, s, neg)
        s_fold = jnp.maximum(jnp.maximum(s[:, :BLOCK], s[:, BLOCK:2 * BLOCK]), s[:, 2 * BLOCK:])
        m = jnp.maximum(jnp.max(s_fold, axis=-1, keepdims=True), sink)
        p = jnp.exp2(s - m)
        p_fold = p[:, :BLOCK] + p[:, BLOCK:2 * BLOCK] + p[:, 2 * BLOCK:]
        l = jnp.sum(p_fold, axis=-1, keepdims=True) + jnp.exp2(sink - m)
        outs[sub, g, e] = _dot(_bf(p), window(vp_ref, vc_ref, vn_ref, sub, sl)) * (1.0 / l)
    for sub in range(nsub):
        qs = slice(sub * BLOCK, (sub + 1) * BLOCK)
        for g in range(2):
            o = jnp.where(low, outs[sub, g, 0], outs[sub, g, 1])
            for jj in range(2):
                j = 2 * g + jj
                o_ref[qs, j * LANES:(j + 1) * LANES] = _bf(
                    o[jj * BLOCK:(jj + 1) * BLOCK, :] * gate_ref[qs, j * LANES:(j + 1) * LANES])


def _attn_a_call(qa, ka4, va4, gates, sink, layer, B, S, nsub):
    T = qa.shape[0]
    nb = S // BLOCK
    nstep = nb // nsub
    tq = nsub * BLOCK
    cur = lambda b, n: (b * nstep + n, 0)
    prv = lambda b, n: (b * nb + jnp.maximum(nsub * n - 1, 0), 0)
    nxt = lambda b, n: (b * nb + jnp.minimum(nsub * n + nsub, nb - 1), 0)
    w = 4 * LANES
    return pl.pallas_call(
        functools.partial(_attn_a_kernel, nb=nb, layer=layer),
        out_shape=jax.ShapeDtypeStruct((T, A_WIDTH), jnp.bfloat16),
        grid=(B, nstep),
        in_specs=[
            pl.BlockSpec(memory_space=pltpu.SMEM),
            pl.BlockSpec((tq, A_WIDTH), cur),
            pl.BlockSpec((BLOCK, w), prv), pl.BlockSpec((tq, w), cur), pl.BlockSpec((BLOCK, w), nxt),
            pl.BlockSpec((BLOCK, w), prv), pl.BlockSpec((tq, w), cur), pl.BlockSpec((BLOCK, w), nxt),
            pl.BlockSpec((tq, A_WIDTH), cur),
        ],
        out_specs=pl.BlockSpec((tq, A_WIDTH), cur),
        compiler_params=pltpu.CompilerParams(
            dimension_semantics=("arbitrary", "arbitrary"), vmem_limit_bytes=VMEM_LIMIT),
        name="attn_a",
    )(sink, qa, ka4, ka4, ka4, va4, va4, va4, gates)


def _attn_b_kernel(q_ref, k_ref, vt_ref, gate_ref, o_ref, *, tk, lag):
    S = k_ref.shape[0]
    tq = q_ref.shape[0]
    nk = S // tk
    nh = q_ref.shape[1] // LANES
    outs = []
    for hh in range(nh):
        qt = _bf(q_ref[:, hh * LANES:(hh + 1) * LANES].astype(jnp.float32).T)
        m_run = jnp.full((1, tq), -1e30, jnp.float32)
        acc = jnp.zeros((VT_ROWS, tq), jnp.float32)
        inflight = []
        for c in range(nk + lag):
            if c < nk:
                ks = k_ref[c * tk:(c + 1) * tk, hh * LANES:(hh + 1) * LANES]
                inflight.append(_dot(ks, qt))
            if c >= lag:
                cc = c - lag
                st = inflight.pop(0)
                m_new = jnp.maximum(m_run, jnp.max(st, axis=0, keepdims=True))
                alpha = jnp.exp2(m_run - m_new)
                p = _bf(jnp.exp2(st - m_new))
                vt = vt_ref[0, hh * VT_ROWS:(hh + 1) * VT_ROWS, cc * tk:(cc + 1) * tk]
                acc = acc * alpha + _dot(vt, p)
                m_run = m_new
        outs.append(acc[:B_V_DIM, :] * (1.0 / acc[B_V_DIM:B_V_DIM + 1, :]))
    o = jnp.concatenate(outs, axis=0).T
    o_ref[...] = _bf(o * gate_ref[...])


def _attn_b_call(qb, kb, vt, gates, B, S, tq, tk, lag, nh):
    T = qb.shape[0]
    nq = S // tq
    ngrp = B_HEADS // nh
    ow = nh * B_V_DIM
    gate_col0 = A_WIDTH // ow
    return pl.pallas_call(
        functools.partial(_attn_b_kernel, tk=tk, lag=lag),
        out_shape=jax.ShapeDtypeStruct((T, B_WIDTH), jnp.bfloat16),
        grid=(B, ngrp, nq),
        in_specs=[
            pl.BlockSpec((tq, nh * LANES), lambda b, j, i: (b * nq + i, j)),
            pl.BlockSpec((S, nh * LANES), lambda b, j, i: (b, j)),
            pl.BlockSpec((1, nh * VT_ROWS, S), lambda b, j, i: (b, j, 0)),
            pl.BlockSpec((tq, ow), lambda b, j, i: (b * nq + i, gate_col0 + j)),
        ],
        out_specs=pl.BlockSpec((tq, ow), lambda b, j, i: (b * nq + i, j)),
        compiler_params=pltpu.CompilerParams(
            dimension_semantics=("arbitrary", "arbitrary", "arbitrary"), vmem_limit_bytes=VMEM_LIMIT),
        name="attn_b",
    )(qb, kb, vt, gates)


def _out_kernel(x_ref, oa_ref, ob_ref, wout_ref, pg_ref, wgate_ref, p_ref, wproj_ref, y_ref):
    mix = jnp.concatenate([oa_ref[...], ob_ref[...]], axis=-1)
    x1 = x_ref[...] + _dot(mix, wout_ref[...])
    hn = _rms(x1, D_MODEL) * pg_ref[...]
    gate = jax.nn.sigmoid(_dot(_bf(hn), wgate_ref[...]))
    y_ref[...] = x1 + gate * _dot(_bf(p_ref[...]), wproj_ref[...])


def _out_call(x2d, oa, ob, layer, wout, pg, wgate, p3d, wproj, tm):
    T = x2d.shape[0]
    per_layer = lambda w: _layer_spec(w, layer)
    rows = lambda w: pl.BlockSpec((tm, w), lambda i: (i, 0))
    return pl.pallas_call(
        _out_kernel,
        out_shape=jax.ShapeDtypeStruct((T, D_MODEL), jnp.float32),
        grid=(T // tm,),
        in_specs=[rows(D_MODEL), rows(A_WIDTH), rows(B_WIDTH), per_layer(wout),
                  per_layer(pg), per_layer(wgate),
                  pl.BlockSpec((None, tm, PLE_DIM), lambda i: (layer, i, 0)), per_layer(wproj)],
        out_specs=rows(D_MODEL),
        compiler_params=pltpu.CompilerParams(
            dimension_semantics=("arbitrary",), vmem_limit_bytes=VMEM_LIMIT),
        name="out_proj",
    )(x2d, oa, ob, wout, pg, wgate, p3d, wproj)


def _prep_weights(w_in, b_w_uq, b_w_ukv, a_q_norm, a_k_norm, b_q_norm, b_k_norm):
    depth = w_in.shape[0]
    aq, ak, av, ag, bcq, bckv, bkr, bg = jnp.split(w_in, [512, 640, 768, 1280, 1664, 1920, 1952], axis=-1)
    aq_src = np.concatenate([j * LANES + A_SRC for j in range(A_WIDTH // LANES)])
    kr_src = np.where(B_ROPE_LANE, B_SRC - B_NOPE, -1)
    wa = _bf(jnp.concatenate([_take(aq, aq_src), _take(ak, A_SRC), av], axis=-1))
    wc = _bf(jnp.concatenate([bcq, bckv, _take(bkr, kr_src)], axis=-1))
    wg = _bf(jnp.concatenate([ag, bg], axis=-1))
    uq = _take(b_w_uq.reshape(depth, B_Q_LORA, B_HEADS, B_QK_DIM), B_SRC)
    wuq = _bf(uq.reshape(depth, B_Q_LORA, B_HEADS * LANES))
    uq_rot = jnp.where(jnp.asarray(B_ROPE_LANE), jnp.roll(uq, HALF, axis=-1), 0.0)
    wuqr = _bf(uq_rot.reshape(depth, B_Q_LORA, B_HEADS * LANES))
    ukv = b_w_ukv.reshape(depth, B_KV_LORA, B_HEADS, B_NOPE + B_V_DIM)
    kn_src = np.where(B_ROPE_LANE, -1, B_SRC)
    kn = _take(ukv[..., :B_NOPE], kn_src).reshape(depth, B_KV_LORA, B_HEADS * LANES)
    wukv = _bf(jnp.concatenate([kn, ukv[..., B_NOPE:].reshape(depth, B_KV_LORA, B_WIDTH)], axis=-1))

    qscale = (A_HEAD_DIM ** -0.5) * LOG2E
    bscale = (B_QK_DIM ** -0.5) * LOG2E
    gaq = _take(a_q_norm, A_GAIN) * qscale
    gak = _take(a_k_norm, A_GAIN)
    gbq = _take(b_q_norm, B_SRC) * bscale
    gbk = _take(b_k_norm, B_SRC)
    rolled = lambda g: jnp.roll(g, HALF, axis=-1)
    rolled_b = lambda g: jnp.where(jnp.asarray(B_ROPE_LANE), rolled(g), 0.0)
    rows = jnp.stack([gaq, rolled(gaq), gak, rolled(gak), gbq, rolled_b(gbq), gbk, rolled_b(gbk)], axis=1)
    return wc, wa, wg, wuq, wuqr, wukv, rows


def kernel(x, p, positions, norm_g, w_in, a_q_norm, a_k_norm, a_sink, b_cq_norm, b_ckv_norm,
           b_w_uq, b_w_ukv, b_q_norm, b_k_norm, w_out, ple_g, ple_w_gate, ple_w_proj):
    B, S, _ = x.shape
    depth = w_in.shape[0]
    T = B * S

    tabs = _rope_tables(positions)
    wc, wa, wg, wuq, wuqr, wukv, rows = _prep_weights(w_in, b_w_uq, b_w_ukv, a_q_norm, a_k_norm, b_q_norm, b_k_norm)
    lane = np.arange(2 * LANES)
    same_head = ((lane[:, None] // LANES) == (lane[None, :] // LANES)) & \
                (((lane[:, None] // 32) % 2) == ((lane[None, :] // 32) % 2))
    g256 = jnp.asarray(same_head, jnp.bfloat16)
    g128 = g256[:LANES, :LANES]
    wout_b, wgate_b, wproj_b = _bf(w_out), _bf(ple_w_gate), _bf(ple_w_proj)

    ng, cqg, ckvg, pg = (v[:, None, :] for v in (norm_g, b_cq_norm, b_ckv_norm, ple_g))
    sink = a_sink.astype(jnp.float32)
    xc = x.reshape(T, D_MODEL)
    p3 = p.reshape(depth, T, PLE_DIM)
    for i in range(depth):
        qa, ka4, va4, gates, qb, kb, vt = _proj_call(
            xc, B, S, i, ng, wc, wa, wg, tabs, rows, cqg, ckvg, wuq, wuqr, wukv, g256, g128, tm=PROJ_ROWS)
        oa = _attn_a_call(qa, ka4, va4, gates, sink, i, B, S, nsub=A_BLOCKS_PER_STEP)
        ob = _attn_b_call(qb, kb, vt, gates, B, S, tq=B_QUERIES_PER_STEP, tk=B_KEY_CHUNK,
                          lag=B_SCORE_LEAD, nh=B_HEADS_PER_STEP)
        xc = _out_call(xc, oa, ob, i, wout_b, pg, wgate_b, p3, wproj_b, tm=OUT_ROWS)
    return xc.reshape(B, S, D_MODEL)
```
